```python
import math
import jax, jax.numpy as jnp
from jax import lax
import numpy as np

D_MODEL = 1024
BATCH = 4
SEQ = 8192
DEPTH = 2
DEC_BATCH = 16
DEC_SEQ = 2048
PAST_LEN = 128

N_MIXERS = 2
N_CONF = (DEPTH + 1) // 2
N_HYENA = DEPTH // 2
CONV_WIDTH = 31
SHORT_WIDTH = 3
HYENA_ORDER = 2
EMB_DIM = 33
FILTER_ORDER = 64
N_INNER_MLPS = 2
FAST_DECAY_PCT = 0.3
SLOW_DECAY_PCT = 1.5
DECAY_TARGET = 1e-2
N_EXPERTS = 16
CAPACITY_FACTOR = 2
D_EXPERT = 2816
RMS_EPS = 1e-6
LN_EPS = 1e-5

kernel_name = 'conformer_hyena_ec_moe_encoder'


def _rms_norm(x, g):
    xf = x.astype(jnp.float32)
    y = xf * lax.rsqrt(jnp.mean(xf * xf, axis=-1, keepdims=True) + RMS_EPS)
    return (y * g.astype(jnp.float32)).astype(x.dtype)


def _layer_norm(x, g, b):
    xf = x.astype(jnp.float32)
    mu = jnp.mean(xf, axis=-1, keepdims=True)
    xc = xf - mu
    var = jnp.mean(xc * xc, axis=-1, keepdims=True)
    y = xc * lax.rsqrt(var + LN_EPS) * g.astype(jnp.float32) + b.astype(jnp.float32)
    return y.astype(x.dtype)


def _depthwise_conv(x, w, b):
    k = w.shape[0]
    c = x.shape[-1]
    y = lax.conv_general_dilated(
        x, w.astype(x.dtype)[:, None, :], window_strides=(1,),
        padding=[(k // 2, k // 2)], dimension_numbers=('NWC', 'WIO', 'NWC'),
        feature_group_count=c)
    return y + b.astype(x.dtype)


def _conformer_conv(h, w_pw1, b_pw1, w_dw, b_dw, ln_g, ln_b, w_pw2, b_pw2):
    u = h @ w_pw1 + b_pw1
    a, g = jnp.split(u, 2, axis=-1)
    u = a * jax.nn.sigmoid(g)
    u = _depthwise_conv(u, w_dw, b_dw)
    u = jax.nn.silu(_layer_norm(u, ln_g, ln_b))
    return u @ w_pw2 + b_pw2


def _hyena_filter(L, w_f0, b_f0, w_fin, b_fin, w_fout, sin_freq):
    f32 = jnp.float32
    d = w_fout.shape[-1] // 2
    t = jnp.linspace(0.0, 1.0, L, dtype=f32)[:, None]
    bands = (EMB_DIM - 1) // 2
    f = jnp.linspace(1e-4, bands - 1, bands, dtype=f32)[None, :]
    w = 2.0 * math.pi * jnp.arange(L, dtype=f32)[:, None] / L
    z = jnp.concatenate([t, jnp.cos(f * w), -jnp.sin(f * w)], axis=-1)
    freq = sin_freq.astype(f32)
    hf = jnp.sin(freq * (z @ w_f0.astype(f32) + b_f0.astype(f32)))
    for i in range(N_INNER_MLPS):
        hf = jnp.sin(freq * (hf @ w_fin[i].astype(f32) + b_fin[i].astype(f32)))
    hf = hf @ w_fout.astype(f32)
    max_decay = math.log(DECAY_TARGET) / FAST_DECAY_PCT
    min_decay = math.log(DECAY_TARGET) / SLOW_DECAY_PCT
    deltas = jnp.linspace(min_decay, max_decay, d, dtype=f32)[None, :]
    decay = jnp.exp(-t * jnp.abs(deltas))
    h_fwd = hf[:, :d] * decay
    h_bwd = hf[:, d:] * decay
    k = jnp.concatenate([h_fwd, jnp.zeros((1, d), f32), h_bwd[:0:-1]], axis=0)
    return k / jnp.sum(jnp.abs(k), axis=0, keepdims=True)


def _hyena(h, w_in, b_in, w_short, b_short, w_f0, b_f0, w_fin, b_fin, w_fout,
           sin_freq, fft_bias, w_out, b_out):
    L = h.shape[1]
    u = h @ w_in + b_in
    u = _depthwise_conv(u, w_short, b_short)
    x0, x1, v = jnp.split(u, HYENA_ORDER + 1, axis=-1)
    v = (v * x1).astype(jnp.float32)
    k = _hyena_filter(L, w_f0, b_f0, w_fin, b_fin, w_fout, sin_freq)
    n = 2 * L
    y = jnp.fft.irfft(jnp.fft.rfft(v, n=n, axis=1) * jnp.fft.rfft(k, n=n, axis=0)[None],
                      n=n, axis=1)[:, :L]
    y = y + v * fft_bias.astype(jnp.float32)
    y = y.astype(h.dtype) * x0
    return y @ w_out + b_out


def _ec_moe(h, w_router, w_gate, w_up, w_down):
    b, l, d = h.shape
    t = b * l
    xf = h.reshape(t, d)
    cap = max(1, min(t, CAPACITY_FACTOR * t // N_EXPERTS))
    probs = jax.nn.softmax((xf @ w_router).astype(jnp.float32), axis=-1)
    gates, idx = lax.top_k(probs.T, cap)
    xs = xf[idx]
    hid = jax.nn.silu(jnp.einsum('ecd,edf->ecf', xs, w_gate)) * jnp.einsum('ecd,edf->ecf', xs, w_up)
    out = jnp.einsum('ecf,efd->ecd', hid, w_down) * gates[..., None].astype(h.dtype)
    y = jnp.zeros_like(xf).at[idx.reshape(-1)].add(out.reshape(-1, d))
    return y.reshape(b, l, d)


def _encoder(x, norm_mix_g, norm_ffn_g, final_norm_g,
             conf_w_pw1, conf_b_pw1, conf_w_dw, conf_b_dw, conf_ln_g, conf_ln_b,
             conf_w_pw2, conf_b_pw2,
             hy_w_in, hy_b_in, hy_w_short, hy_b_short, hy_w_f0, hy_b_f0, hy_w_fin,
             hy_b_fin, hy_w_fout, hy_sin_freq, hy_fft_bias, hy_w_out, hy_b_out,
             moe_w_router, moe_w_gate, moe_w_up, moe_w_down):
    for i in range(DEPTH):
        hn = _rms_norm(x, norm_mix_g[i])
        j = i // N_MIXERS
        if i % N_MIXERS == 0:
            mix = _conformer_conv(hn, conf_w_pw1[j], conf_b_pw1[j], conf_w_dw[j], conf_b_dw[j],
                                  conf_ln_g[j], conf_ln_b[j], conf_w_pw2[j], conf_b_pw2[j])
        else:
            mix = _hyena(hn, hy_w_in[j], hy_b_in[j], hy_w_short[j], hy_b_short[j],
                         hy_w_f0[j], hy_b_f0[j], hy_w_fin[j], hy_b_fin[j], hy_w_fout[j],
                         hy_sin_freq[j], hy_fft_bias[j], hy_w_out[j], hy_b_out[j])
        x = x + mix
        x = x + _ec_moe(_rms_norm(x, norm_ffn_g[i]), moe_w_router[i], moe_w_gate[i],
                        moe_w_up[i], moe_w_down[i])
    return _rms_norm(x, final_norm_g)


def _nrm(key, shape, scale):
    return jax.random.normal(key, shape, jnp.float32) * scale


def setup_inputs(seed: int = 0) -> dict:
    key = jax.random.key(seed)
    ks = list(jax.random.split(key, 32))
    D, E, F, O = D_MODEL, N_EXPERTS, D_EXPERT, FILTER_ORDER
    A, H = N_CONF, N_HYENA
    inp = {}
    inp['x_prompt'] = _nrm(ks[0], (BATCH, SEQ, D), 1.0)
    inp['x_sample'] = _nrm(ks[1], (DEC_BATCH, DEC_SEQ, D), 1.0)
    inp['norm_mix_g'] = 1.0 + _nrm(ks[2], (DEPTH, D), 0.02)
    inp['norm_ffn_g'] = 1.0 + _nrm(ks[3], (DEPTH, D), 0.02)
    inp['final_norm_g'] = 1.0 + _nrm(ks[4], (D,), 0.02)
    inp['conf_w_pw1'] = _nrm(ks[5], (A, D, 2 * D), D ** -0.5)
    inp['conf_b_pw1'] = _nrm(ks[6], (A, 2 * D), 0.02)
    inp['conf_w_dw'] = _nrm(ks[7], (A, CONV_WIDTH, D), CONV_WIDTH ** -0.5)
    inp['conf_b_dw'] = _nrm(ks[8], (A, D), 0.02)
    inp['conf_ln_g'] = 1.0 + _nrm(ks[9], (A, D), 0.02)
    inp['conf_ln_b'] = _nrm(ks[10], (A, D), 0.02)
    inp['conf_w_pw2'] = _nrm(ks[11], (A, D, D), D ** -0.5)
    inp['conf_b_pw2'] = _nrm(ks[12], (A, D), 0.02)
    inp['hy_w_in'] = _nrm(ks[13], (H, D, (HYENA_ORDER + 1) * D), D ** -0.5)
    inp['hy_b_in'] = _nrm(ks[14], (H, (HYENA_ORDER + 1) * D), 0.02)
    inp['hy_w_short'] = _nrm(ks[15], (H, SHORT_WIDTH, (HYENA_ORDER + 1) * D), SHORT_WIDTH ** -0.5)
    inp['hy_b_short'] = _nrm(ks[16], (H, (HYENA_ORDER + 1) * D), 0.02)
    inp['hy_w_f0'] = _nrm(ks[17], (H, EMB_DIM, O), EMB_DIM ** -0.5)
    inp['hy_b_f0'] = _nrm(ks[18], (H, O), 0.1)
    inp['hy_w_fin'] = _nrm(ks[19], (H, N_INNER_MLPS, O, O), O ** -0.5)
    inp['hy_b_fin'] = _nrm(ks[20], (H, N_INNER_MLPS, O), 0.1)
    inp['hy_w_fout'] = _nrm(ks[21], (H, O, 2 * D), O ** -0.5)
    inp['hy_sin_freq'] = 1.0 + _nrm(ks[22], (H, O), 0.02)
    inp['hy_fft_bias'] = _nrm(ks[23], (H, D), 1.0)
    inp['hy_w_out'] = _nrm(ks[24], (H, D, D), D ** -0.5)
    inp['hy_b_out'] = _nrm(ks[25], (H, D), 0.02)
    inp['moe_w_router'] = _nrm(ks[26], (DEPTH, D, E), D ** -0.5)
    inp['moe_w_gate'] = _nrm(ks[27], (DEPTH, E, D, F), D ** -0.5)
    inp['moe_w_up'] = _nrm(ks[28], (DEPTH, E, D, F), D ** -0.5)
    inp['moe_w_down'] = _nrm(ks[29], (DEPTH, E, F, D), F ** -0.5)
    return inp


def reference(x_prompt, x_sample, norm_mix_g, norm_ffn_g, final_norm_g,
              conf_w_pw1, conf_b_pw1, conf_w_dw, conf_b_dw, conf_ln_g, conf_ln_b,
              conf_w_pw2, conf_b_pw2,
              hy_w_in, hy_b_in, hy_w_short, hy_b_short, hy_w_f0, hy_b_f0, hy_w_fin,
              hy_b_fin, hy_w_fout, hy_sin_freq, hy_fft_bias, hy_w_out, hy_b_out,
              moe_w_router, moe_w_gate, moe_w_up, moe_w_down):
    weights = (norm_mix_g, norm_ffn_g, final_norm_g,
               conf_w_pw1, conf_b_pw1, conf_w_dw, conf_b_dw, conf_ln_g, conf_ln_b,
               conf_w_pw2, conf_b_pw2,
               hy_w_in, hy_b_in, hy_w_short, hy_b_short, hy_w_f0, hy_b_f0, hy_w_fin,
               hy_b_fin, hy_w_fout, hy_sin_freq, hy_fft_bias, hy_w_out, hy_b_out,
               moe_w_router, moe_w_gate, moe_w_up, moe_w_down)
    y_prompt = _encoder(x_prompt, *weights)
    y_sample = _encoder(x_sample, *weights)
    return (y_prompt, y_sample)
```

```python
import functools
import math

import jax
import jax.numpy as jnp
from jax import lax
from jax.experimental import pallas as pl
from jax.experimental.pallas import tpu as pltpu

F32 = jnp.float32
BF16 = jnp.bfloat16
I32 = jnp.int32

RMS_EPS = 1e-6
LN_EPS = 1e-5
CONV_WIDTH = 31
SHORT_WIDTH = 3
EMB_DIM = 33
N_INNER_MLPS = 2
FAST_DECAY_PCT = 0.3
SLOW_DECAY_PCT = 1.5
DECAY_TARGET = 1e-2
N_EXPERTS = 16
CAPACITY_FACTOR = 2

LANE = 128
BF16_SUBLANE_TILE = 16
MXU_DIM = 256
VMEM_LIMIT = 56 * 1024 * 1024

ROW_TILE = 512
DFT_B = 128
GATHER_BLK = 256
GATHER_WIN = GATHER_BLK + BF16_SUBLANE_TILE
COMB_BLK = 512
COMB_WIN = 128
FFN_ROWS = 512
F_CHUNK = 256


def _cparams(sem):
    return pltpu.CompilerParams(dimension_semantics=sem, vmem_limit_bytes=VMEM_LIMIT)


def _rms(x, g):
    return x * lax.rsqrt(jnp.mean(x * x, axis=-1, keepdims=True) + RMS_EPS) * g


def _norm_proj_kernel(x_ref, g_ref, w_ref, b_ref, o_ref):
    h = _rms(x_ref[...], g_ref[...]).astype(BF16)
    o_ref[...] = jnp.dot(h, w_ref[...], preferred_element_type=F32) + b_ref[...]


def _norm_glu_kernel(x_ref, g_ref, wa_ref, wg_ref, ba_ref, bg_ref, o_ref):
    h = _rms(x_ref[...], g_ref[...]).astype(BF16)
    a = jnp.dot(h, wa_ref[...], preferred_element_type=F32) + ba_ref[...]
    g = jnp.dot(h, wg_ref[...], preferred_element_type=F32) + bg_ref[...]
    o_ref[...] = a * jax.nn.sigmoid(g)


def _norm_proj(x, g, w, b, tn):
    t, d = x.shape
    n = w.shape[1]
    return pl.pallas_call(
        _norm_proj_kernel,
        grid=(t // ROW_TILE, n // tn),
        in_specs=[
            pl.BlockSpec((ROW_TILE, d), lambda i, j: (i, 0)),
            pl.BlockSpec((1, d), lambda i, j: (0, 0)),
            pl.BlockSpec((d, tn), lambda i, j: (0, j)),
            pl.BlockSpec((1, tn), lambda i, j: (0, j)),
        ],
        out_specs=pl.BlockSpec((ROW_TILE, tn), lambda i, j: (i, j)),
        out_shape=jax.ShapeDtypeStruct((t, n), F32),
        compiler_params=_cparams(("parallel", "parallel")),
        name="norm_proj",
    )(x, g.reshape(1, d), w, b.reshape(1, n))


def _norm_glu(x, g, w, b):
    t, d = x.shape
    n = w.shape[1] // 2
    return pl.pallas_call(
        _norm_glu_kernel,
        grid=(t // ROW_TILE,),
        in_specs=[
            pl.BlockSpec((ROW_TILE, d), lambda i: (i, 0)),
            pl.BlockSpec((1, d), lambda i: (0, 0)),
            pl.BlockSpec((d, n), lambda i: (0, 0)),
            pl.BlockSpec((d, n), lambda i: (0, 1)),
            pl.BlockSpec((1, n), lambda i: (0, 0)),
            pl.BlockSpec((1, n), lambda i: (0, 1)),
        ],
        out_specs=pl.BlockSpec((ROW_TILE, n), lambda i: (i, 0)),
        out_shape=jax.ShapeDtypeStruct((t, n), F32),
        compiler_params=_cparams(("parallel",)),
        name="norm_glu",
    )(x, g.reshape(1, d), w, w, b.reshape(1, 2 * n), b.reshape(1, 2 * n))


HALO = 16


def _conf_tail_kernel(u_ref, up_ref, un_ref, x_ref, wdw_ref, bdw_ref, lng_ref, lnb_ref,
                      w2_ref, b2_ref, o_ref, ext_ref, *, tm, nblk):
    i = pl.program_id(1)
    prev = up_ref[...]
    nxt = un_ref[...]
    ext_ref[pl.ds(0, HALO), :] = jnp.where(i > 0, prev, jnp.zeros_like(prev))
    ext_ref[pl.ds(HALO, tm), :] = u_ref[...]
    ext_ref[pl.ds(HALO + tm, HALO), :] = jnp.where(i < nblk - 1, nxt, jnp.zeros_like(nxt))
    off = HALO - CONV_WIDTH // 2
    acc = ext_ref[pl.ds(off, tm), :] * wdw_ref[pl.ds(0, 1), :] + bdw_ref[...]
    for k in range(1, CONV_WIDTH):
        acc = acc + ext_ref[pl.ds(off + k, tm), :] * wdw_ref[pl.ds(k, 1), :]
    mu = jnp.mean(acc, axis=-1, keepdims=True)
    xc = acc - mu
    var = jnp.mean(xc * xc, axis=-1, keepdims=True)
    y = xc * lax.rsqrt(var + LN_EPS) * lng_ref[...] + lnb_ref[...]
    y = (y * jax.nn.sigmoid(y)).astype(BF16)
    o_ref[...] = jnp.dot(y, w2_ref[...], preferred_element_type=F32) + b2_ref[...] + x_ref[...]


def _conf_tail(u, x, w_dw, b_dw, ln_g, ln_b, w2, b2):
    bt, l, d = u.shape
    tm = ROW_TILE
    nblk = l // tm
    hb = tm // HALO
    wpad = jnp.zeros((32, d), F32).at[:CONV_WIDTH].set(w_dw)
    row = lambda a: a.reshape(1, d)
    return pl.pallas_call(
        functools.partial(_conf_tail_kernel, tm=tm, nblk=nblk),
        grid=(bt, nblk),
        in_specs=[
            pl.BlockSpec((None, tm, d), lambda b, i: (b, i, 0)),
            pl.BlockSpec((None, HALO, d), lambda b, i: (b, jnp.maximum(i * hb - 1, 0), 0)),
            pl.BlockSpec((None, HALO, d), lambda b, i: (b, jnp.minimum((i + 1) * hb, nblk * hb - 1), 0)),
            pl.BlockSpec((None, tm, d), lambda b, i: (b, i, 0)),
            pl.BlockSpec((32, d), lambda b, i: (0, 0)),
            pl.BlockSpec((1, d), lambda b, i: (0, 0)),
            pl.BlockSpec((1, d), lambda b, i: (0, 0)),
            pl.BlockSpec((1, d), lambda b, i: (0, 0)),
            pl.BlockSpec((d, d), lambda b, i: (0, 0)),
            pl.BlockSpec((1, d), lambda b, i: (0, 0)),
        ],
        out_specs=pl.BlockSpec((None, tm, d), lambda b, i: (b, i, 0)),
        out_shape=jax.ShapeDtypeStruct((bt, l, d), F32),
        scratch_shapes=[pltpu.VMEM((tm + 2 * HALO, d), F32)],
        compiler_params=_cparams(("parallel", "parallel")),
        name="conf_tail",
    )(u, u, u, x, wpad, row(b_dw), row(ln_g), row(ln_b), w2, row(b2))


SHALO = 8


def _short_conv_kernel(u_ref, up_ref, un_ref, w_ref, b_ref, vg_ref, x0_ref, ext_ref, *, tm, nblk, d):
    i = pl.program_id(1)
    prev = up_ref[...]
    nxt = un_ref[...]
    ext_ref[pl.ds(0, SHALO), :] = jnp.where(i > 0, prev, jnp.zeros_like(prev))
    ext_ref[pl.ds(SHALO, tm), :] = u_ref[...]
    ext_ref[pl.ds(SHALO + tm, SHALO), :] = jnp.where(i < nblk - 1, nxt, jnp.zeros_like(nxt))

    def conv(c):
        cols = pl.ds(c * d, d)
        acc = b_ref[:, cols]
        for k in range(SHORT_WIDTH):
            acc = acc + ext_ref[pl.ds(SHALO - 1 + k, tm), cols] * w_ref[pl.ds(k, 1), cols]
        return acc

    x0_ref[...] = conv(0)
    vg_ref[...] = conv(2) * conv(1)


def _short_conv(u3, w_short, b_short):
    bt, l, d3 = u3.shape
    d = d3 // 3
    tm = ROW_TILE
    nblk = l // tm
    hb = tm // SHALO
    wpad = jnp.zeros((8, d3), F32).at[:SHORT_WIDTH].set(w_short)
    out = jax.ShapeDtypeStruct((bt, l, d), F32)
    return pl.pallas_call(
        functools.partial(_short_conv_kernel, tm=tm, nblk=nblk, d=d),
        grid=(bt, nblk),
        in_specs=[
            pl.BlockSpec((None, tm, d3), lambda b, i: (b, i, 0)),
            pl.BlockSpec((None, SHALO, d3), lambda b, i: (b, jnp.maximum(i * hb - 1, 0), 0)),
            pl.BlockSpec((None, SHALO, d3), lambda b, i: (b, jnp.minimum((i + 1) * hb, nblk * hb - 1), 0)),
            pl.BlockSpec((8, d3), lambda b, i: (0, 0)),
            pl.BlockSpec((1, d3), lambda b, i: (0, 0)),
        ],
        out_specs=[pl.BlockSpec((None, tm, d), lambda b, i: (b, i, 0))] * 2,
        out_shape=[out, out],
        scratch_shapes=[pltpu.VMEM((tm + 2 * SHALO, d3), F32)],
        compiler_params=_cparams(("parallel", "parallel")),
        name="short_conv",
    )(u3, u3, u3, wpad, b_short.reshape(1, d3))


def _dft1_kernel(v_ref, fa_ref, twr_ref, twi_ref, xr_ref, xi_ref, *, bt, a):
    for j in range(bt):
        r = jnp.dot(fa_ref[...], v_ref[j], preferred_element_type=F32)
        re, im = r[:a], r[a:]
        twr, twi = twr_ref[j], twi_ref[j]
        xr_ref[j] = (re * twr - im * twi).astype(BF16)
        xi_ref[j] = (re * twi + im * twr).astype(BF16)


def _dft2_kernel(xr_ref, xi_ref, mf_ref, mi_ref, kr_ref, ki_ref, cr_ref, ci_ref, zr_ref, zi_ref, *, at, b):
    for j in range(at):
        x = jnp.concatenate([xr_ref[j], xi_ref[j]], axis=0)
        v = jnp.dot(mf_ref[...], x, preferred_element_type=F32)
        vr, vi = v[:b], v[b:]
        kr, ki = kr_ref[j], ki_ref[j]
        yr = (vr * kr - vi * ki).astype(BF16)
        yi = (vr * ki + vi * kr).astype(BF16)
        z = jnp.dot(mi_ref[...], jnp.concatenate([yr, yi], axis=0), preferred_element_type=F32)
        zr, zi = z[:b], z[b:]
        cr, ci = cr_ref[j], ci_ref[j]
        zr_ref[j] = (zr * cr - zi * ci).astype(BF16)
        zi_ref[j] = (zr * ci + zi * cr).astype(BF16)


def _dft3_kernel(zr_ref, zi_ref, fi_ref, y_ref, *, bt):
    for j in range(bt):
        z = jnp.concatenate([zr_ref[j], zi_ref[j]], axis=0)
        y_ref[j] = jnp.dot(fi_ref[...], z, preferred_element_type=F32)


def _long_conv(vg, kspec):
    bt, l, d = vg.shape
    n = 2 * l
    b = DFT_B
    a = n // b
    ah = a // 2
    two_pi = 2.0 * math.pi

    f1 = jnp.arange(a, dtype=F32)[:, None]
    aa = jnp.arange(ah, dtype=F32)[None, :]
    ang_a = two_pi * ((f1 * aa) % a) / a
    fa = jnp.concatenate([jnp.cos(ang_a), -jnp.sin(ang_a)], axis=0).astype(BF16)
    fi = jnp.concatenate([jnp.cos(ang_a).T, -jnp.sin(ang_a).T], axis=1).astype(BF16)
    bb = jnp.arange(b, dtype=F32)
    ang_b = two_pi * ((bb[:, None] * bb[None, :]) % b) / b
    cb, sb = jnp.cos(ang_b), jnp.sin(ang_b)
    mf = jnp.concatenate([jnp.concatenate([cb, sb], 1), jnp.concatenate([-sb, cb], 1)], 0).astype(BF16)
    mi = jnp.concatenate([jnp.concatenate([cb, -sb], 1), jnp.concatenate([sb, cb], 1)], 0).astype(BF16)
    ang_t = two_pi * (bb[:, None] * jnp.arange(a, dtype=F32)[None, :]) / n
    twr, twi = jnp.cos(ang_t)[..., None], -jnp.sin(ang_t)[..., None]
    cr, ci = jnp.cos(ang_t).T[..., None], jnp.sin(ang_t).T[..., None]

    k2 = kspec.reshape(b, a, d).transpose(1, 0, 2)
    kr, ki = jnp.real(k2).astype(F32), jnp.imag(k2).astype(F32)

    vt = vg.reshape(bt, ah, b, d).transpose(0, 2, 1, 3).astype(BF16)

    sb1 = 8
    xs = jax.ShapeDtypeStruct((bt, b, a, d), BF16)
    xr, xi = pl.pallas_call(
        functools.partial(_dft1_kernel, bt=sb1, a=a),
        grid=(bt, b // sb1),
        in_specs=[
            pl.BlockSpec((None, sb1, ah, d), lambda n_, j: (n_, j, 0, 0)),
            pl.BlockSpec((2 * a, ah), lambda n_, j: (0, 0)),
            pl.BlockSpec((sb1, a, 1), lambda n_, j: (j, 0, 0)),
            pl.BlockSpec((sb1, a, 1), lambda n_, j: (j, 0, 0)),
        ],
        out_specs=[pl.BlockSpec((None, sb1, a, d), lambda n_, j: (n_, j, 0, 0))] * 2,
        out_shape=[xs, xs],
        compiler_params=_cparams(("parallel", "parallel")),
        name="dft_stage1",
    )(vt, fa, twr, twi)

    xr = xr.transpose(0, 2, 1, 3)
    xi = xi.transpose(0, 2, 1, 3)

    at = 4
    zs = jax.ShapeDtypeStruct((bt, a, b, d), BF16)
    zr, zi = pl.pallas_call(
        functools.partial(_dft2_kernel, at=at, b=b),
        grid=(a // at, bt),
        in_specs=[
            pl.BlockSpec((None, at, b, d), lambda i, n_: (n_, i, 0, 0)),
            pl.BlockSpec((None, at, b, d), lambda i, n_: (n_, i, 0, 0)),
            pl.BlockSpec((2 * b, 2 * b), lambda i, n_: (0, 0)),
            pl.BlockSpec((2 * b, 2 * b), lambda i, n_: (0, 0)),
            pl.BlockSpec((at, b, d), lambda i, n_: (i, 0, 0)),
            pl.BlockSpec((at, b, d), lambda i, n_: (i, 0, 0)),
            pl.BlockSpec((at, b, 1), lambda i, n_: (i, 0, 0)),
            pl.BlockSpec((at, b, 1), lambda i, n_: (i, 0, 0)),
        ],
        out_specs=[pl.BlockSpec((None, at, b, d), lambda i, n_: (n_, i, 0, 0))] * 2,
        out_shape=[zs, zs],
        compiler_params=_cparams(("parallel", "parallel")),
        name="dft_stage2",
    )(xr, xi, mf, mi, kr, ki, cr, ci)

    zr = zr.transpose(0, 2, 1, 3)
    zi = zi.transpose(0, 2, 1, 3)

    y = pl.pallas_call(
        functools.partial(_dft3_kernel, bt=sb1),
        grid=(bt, b // sb1),
        in_specs=[
            pl.BlockSpec((None, sb1, a, d), lambda n_, j: (n_, j, 0, 0)),
            pl.BlockSpec((None, sb1, a, d), lambda n_, j: (n_, j, 0, 0)),
            pl.BlockSpec((ah, 2 * a), lambda n_, j: (0, 0)),
        ],
        out_specs=pl.BlockSpec((None, sb1, ah, d), lambda n_, j: (n_, j, 0, 0)),
        out_shape=jax.ShapeDtypeStruct((bt, b, ah, d), F32),
        compiler_params=_cparams(("parallel", "parallel")),
        name="dft_stage3",
    )(zr, zi, fi)
    return y.transpose(0, 2, 1, 3).reshape(bt, l, d)


def _hyena_filter_spectrum(l, w_f0, b_f0, w_fin, b_fin, w_fout, sin_freq):
    hp = lax.Precision.HIGHEST
    d = w_fout.shape[-1] // 2
    t = jnp.linspace(0.0, 1.0, l, dtype=F32)[:, None]
    bands = (EMB_DIM - 1) // 2
    f = jnp.linspace(1e-4, bands - 1, bands, dtype=F32)[None, :]
    w = 2.0 * math.pi * jnp.arange(l, dtype=F32)[:, None] / l
    z = jnp.concatenate([t, jnp.cos(f * w), -jnp.sin(f * w)], axis=-1)
    hf = jnp.sin(sin_freq * (jnp.dot(z, w_f0, precision=hp) + b_f0))
    for i in range(N_INNER_MLPS):
        hf = jnp.sin(sin_freq * (jnp.dot(hf, w_fin[i], precision=hp) + b_fin[i]))
    hf = jnp.dot(hf, w_fout, precision=hp)
    max_decay = math.log(DECAY_TARGET) / FAST_DECAY_PCT
    min_decay = math.log(DECAY_TARGET) / SLOW_DECAY_PCT
    deltas = jnp.linspace(min_decay, max_decay, d, dtype=F32)[None, :]
    decay = jnp.exp(-t * jnp.abs(deltas))
    h_fwd = hf[:, :d] * decay
    h_bwd = hf[:, d:] * decay
    k = jnp.concatenate([h_fwd, jnp.zeros((1, d), F32), h_bwd[:0:-1]], axis=0)
    k = k / jnp.sum(jnp.abs(k), axis=0, keepdims=True)
    return jnp.fft.fft(k, axis=0) / (2 * l)


def _hyena_tail_kernel(y_ref, v_ref, x0_ref, fb_ref, w_ref, b_ref, x_ref, o_ref):
    g = ((y_ref[...] + v_ref[...] * fb_ref[...]) * x0_ref[...]).astype(BF16)
    o_ref[...] = jnp.dot(g, w_ref[...], preferred_element_type=F32) + b_ref[...] + x_ref[...]


def _hyena_tail(y, v, x0, fft_bias, w_out, b_out, x):
    t, d = x.shape
    tile = pl.BlockSpec((ROW_TILE, d), lambda i: (i, 0))
    rowv = pl.BlockSpec((1, d), lambda i: (0, 0))
    return pl.pallas_call(
        _hyena_tail_kernel,
        grid=(t // ROW_TILE,),
        in_specs=[tile, tile, tile, rowv, pl.BlockSpec((d, d), lambda i: (0, 0)), rowv, tile],
        out_specs=tile,
        out_shape=jax.ShapeDtypeStruct((t, d), F32),
        compiler_params=_cparams(("parallel",)),
        name="hyena_tail",
    )(y, v, x0, fft_bias.reshape(1, d), w_out, b_out.reshape(1, d), x)


def _router_kernel(x_ref, g_ref, wt_ref, hn_ref, pt_ref):
    h = _rms(x_ref[...], g_ref[...])
    hn_ref[...] = h.astype(BF16)
    lt = lax.dot_general(wt_ref[...], h, (((1,), (1,)), ((), ())), precision=lax.Precision.HIGHEST,
                         preferred_element_type=F32)
    et = jnp.exp(lt - jnp.max(lt, axis=0, keepdims=True))
    pt_ref[...] = et / jnp.sum(et, axis=0, keepdims=True)


def _router(x, g, w_router):
    t, d = x.shape
    e = w_router.shape[1]
    return pl.pallas_call(
        _router_kernel,
        grid=(t // ROW_TILE,),
        in_specs=[
            pl.BlockSpec((ROW_TILE, d), lambda i: (i, 0)),
            pl.BlockSpec((1, d), lambda i: (0, 0)),
            pl.BlockSpec((e, d), lambda i: (0, 0)),
        ],
        out_specs=[
            pl.BlockSpec((ROW_TILE, d), lambda i: (i, 0)),
            pl.BlockSpec((e, ROW_TILE), lambda i: (0, i)),
        ],
        out_shape=[
            jax.ShapeDtypeStruct((t, d), BF16),
            jax.ShapeDtypeStruct((e, t), F32),
        ],
        compiler_params=_cparams(("parallel",)),
        name="router",
    )(x, g.reshape(1, d), w_router.T)


CUM_BLK = 256


def _select_kernel(p_ref, posm_ref, posx_ref, mask_ref, *, cap, t):
    keys = pltpu.bitcast(p_ref[...], I32)
    e = keys.shape[0]

    def vbit(i, tau):
        cand = tau | jnp.left_shift(jnp.int32(1), 30 - i)
        cnt = jnp.sum((keys >= cand).astype(I32), axis=1, keepdims=True)
        return jnp.where(cnt >= cap, cand, tau)

    tau = lax.fori_loop(0, 31, vbit, jnp.zeros((e, 1), I32))
    gt = keys > tau
    eq = keys == tau
    quota = cap - jnp.sum(gt.astype(I32), axis=1, keepdims=True)
    idx = lax.broadcasted_iota(I32, keys.shape, 1)
    nbits = max(1, (t - 1).bit_length())

    def ibit(i, jj):
        cand = jj | jnp.left_shift(jnp.int32(1), nbits - 1 - i)
        cnt = jnp.sum((eq & (idx < cand)).astype(I32), axis=1, keepdims=True)
        return jnp.where(cnt < quota, cand, jj)

    jlast = lax.fori_loop(0, nbits, ibit, jnp.zeros((e, 1), I32))
    mask = gt | (eq & (idx <= jlast))
    mask_ref[...] = mask.astype(F32)

    r = lax.broadcasted_iota(I32, (CUM_BLK, CUM_BLK), 0)
    c = lax.broadcasted_iota(I32, (CUM_BLK, CUM_BLK), 1)
    tri = (r <= c).astype(BF16)
    run = jnp.zeros((e, 1), F32)
    for blk in range(t // CUM_BLK):
        cols = pl.ds(blk * CUM_BLK, CUM_BLK)
        m = mask_ref[:, cols]
        incl = jnp.dot(m.astype(BF16), tri, preferred_element_type=F32)
        pos = (incl - m + run).astype(I32)
        posx_ref[:, cols] = pos
        posm_ref[:, cols] = jnp.where(m > 0.0, pos, -1)
        run = run + incl[:, CUM_BLK - 1:CUM_BLK]


def _select(probs_t, cap):
    e, t = probs_t.shape
    full = pl.BlockSpec((e, t), lambda: (0, 0))
    return pl.pallas_call(
        functools.partial(_select_kernel, cap=cap, t=t),
        in_specs=[full],
        out_specs=[full, full],
        out_shape=[jax.ShapeDtypeStruct((e, t), I32), jax.ShapeDtypeStruct((e, t), I32)],
        scratch_shapes=[pltpu.VMEM((e, t), F32)],
        compiler_params=pltpu.CompilerParams(vmem_limit_bytes=VMEM_LIMIT),
        name="select",
    )(probs_t)


def _ffn_kernel(st_ref, hn_ref, posm_ref, wg_ref, wu_ref, wd_ref, o_ref, xs_ref, acc_ref,
                *, nb, nf, cap, sub):
    e = pl.program_id(0)
    k = pl.program_id(1)

    @pl.when(k == 0)
    def _():
        xs_ref[...] = jnp.zeros_like(xs_ref)

    @pl.when(k < nb)
    def _():
        rows = lax.broadcasted_iota(I32, (GATHER_WIN, GATHER_BLK), 0)
        for j in range(sub):
            base = pl.multiple_of(st_ref[e, k * sub + j], BF16_SUBLANE_TILE)
            rel = posm_ref[:, pl.ds(j * GATHER_BLK, GATHER_BLK)] - base
            onehot = (rows == rel).astype(BF16)
            slab = jnp.dot(onehot, hn_ref[pl.ds(j * GATHER_BLK, GATHER_BLK), :],
                           preferred_element_type=F32)
            win = pl.ds(base, GATHER_WIN)
            xs_ref[win, :] = (xs_ref[win, :].astype(F32) + slab).astype(BF16)

    @pl.when(k >= nb)
    def _():
        for r in range(cap // FFN_ROWS):
            rows = pl.ds(r * FFN_ROWS, FFN_ROWS)
            x = xs_ref[rows, :]
            g = jnp.dot(x, wg_ref[...], preferred_element_type=F32)
            u = jnp.dot(x, wu_ref[...], preferred_element_type=F32)
            h = (g * jax.nn.sigmoid(g) * u).astype(BF16)
            dn = jnp.dot(h, wd_ref[...], preferred_element_type=F32)

            @pl.when(k == nb)
            def _():
                acc_ref[rows, :] = dn

            @pl.when(k > nb)
            def _():
                acc_ref[rows, :] = acc_ref[rows, :] + dn

    @pl.when(k == nb + nf - 1)
    def _():
        o_ref[pl.ds(0, cap), :] = acc_ref[...].astype(BF16)
        o_ref[pl.ds(cap, GATHER_WIN), :] = jnp.zeros((GATHER_WIN, o_ref.shape[1]), BF16)


def _expert_ffn(hn, posm, starts, w_gate, w_up, w_down, cap):
    t, d = hn.shape
    e, _, f = w_gate.shape
    sub = 4
    nb = t // (GATHER_BLK * sub)
    nf = f // F_CHUNK
    cp = cap + GATHER_WIN
    blk = lambda k: jnp.minimum(k, nb - 1)
    chunk = lambda k: jnp.clip(k - nb, 0, nf - 1)
    return pl.pallas_call(
        functools.partial(_ffn_kernel, nb=nb, nf=nf, cap=cap, sub=sub),
        grid_spec=pltpu.PrefetchScalarGridSpec(
            num_scalar_prefetch=1,
            grid=(e, nb + nf),
            in_specs=[
                pl.BlockSpec((GATHER_BLK * sub, d), lambda ei, k, st: (blk(k), 0)),
                pl.BlockSpec((None, 1, GATHER_BLK * sub), lambda ei, k, st: (ei, 0, blk(k))),
                pl.BlockSpec((None, d, F_CHUNK), lambda ei, k, st: (ei, 0, chunk(k))),
                pl.BlockSpec((None, d, F_CHUNK), lambda ei, k, st: (ei, 0, chunk(k))),
                pl.BlockSpec((None, F_CHUNK, d), lambda ei, k, st: (ei, chunk(k), 0)),
            ],
            out_specs=pl.BlockSpec((None, cp, d), lambda ei, k, st: (ei, 0, 0)),
            scratch_shapes=[pltpu.VMEM((cp, d), BF16), pltpu.VMEM((cap, d), F32)],
        ),
        out_shape=jax.ShapeDtypeStruct((e, cp, d), BF16),
        compiler_params=_cparams(("arbitrary", "arbitrary")),
        name="expert_ffn",
    )(starts, hn, posm.reshape(e, 1, t), w_gate, w_up, w_down)


def _combine_kernel(st_ref, nch_ref, x_ref, posm_ref, p_ref, g_ref, out_hbm, o_ref, buf, sem,
                    *, ne, cp, final_norm):
    b = pl.program_id(0)

    def copy(ei, chunk, slot):
        start = pl.multiple_of(ei * cp + st_ref[ei, b] + chunk * COMB_WIN, BF16_SUBLANE_TILE)
        return pltpu.make_async_copy(out_hbm.at[pl.ds(start, COMB_WIN), :], buf.at[slot], sem.at[slot])

    cols = lax.broadcasted_iota(I32, (COMB_BLK, COMB_WIN), 1)

    def contrib(ei, chunk, slot):
        rel = posm_ref[:, ei:ei + 1] - st_ref[ei, b] - chunk * COMB_WIN
        onehot = (cols == rel).astype(BF16)
        return jnp.dot(onehot, buf[slot], preferred_element_type=F32)

    copy(0, 0, 0).start()
    acc = jnp.zeros(o_ref.shape, F32)
    for ei in range(ne):
        slot = ei % 2
        if ei + 1 < ne:
            copy(ei + 1, 0, 1 - slot).start()
        copy(ei, 0, slot).wait()
        part = contrib(ei, 0, slot)

        def extra(chunk, part):
            cpy = copy(ei, chunk, slot)
            cpy.start()
            cpy.wait()
            return part + contrib(ei, chunk, slot)

        part = lax.fori_loop(1, nch_ref[ei, b], extra, part)
        acc = acc + part * p_ref[:, ei:ei + 1]
    y = x_ref[...] + acc
    if final_norm:
        y = _rms(y, g_ref[...])
    o_ref[...] = y


def _combine(x, posm_tok, probs, out_rows, starts, nchunks, final_g):
    t, d = x.shape
    e = posm_tok.shape[1]
    cp = out_rows.shape[0] // e
    final_norm = final_g is not None
    g = final_g if final_norm else jnp.ones((d,), F32)
    return pl.pallas_call(
        functools.partial(_combine_kernel, ne=e, cp=cp, final_norm=final_norm),
        grid_spec=pltpu.PrefetchScalarGridSpec(
            num_scalar_prefetch=2,
            grid=(t // COMB_BLK,),
            in_specs=[
                pl.BlockSpec((COMB_BLK, d), lambda i, st, nc: (i, 0)),
                pl.BlockSpec((COMB_BLK, e), lambda i, st, nc: (i, 0)),
                pl.BlockSpec((COMB_BLK, e), lambda i, st, nc: (i, 0)),
                pl.BlockSpec((1, d), lambda i, st, nc: (0, 0)),
                pl.BlockSpec(memory_space=pl.ANY),
            ],
            out_specs=pl.BlockSpec((COMB_BLK, d), lambda i, st, nc: (i, 0)),
            scratch_shapes=[pltpu.VMEM((2, COMB_WIN, d), BF16), pltpu.SemaphoreType.DMA((2,))],
        ),
        out_shape=jax.ShapeDtypeStruct((t, d), F32),
        compiler_params=_cparams(("arbitrary",)),
        name="combine",
    )(starts, nchunks, x, posm_tok, probs, g.reshape(1, d), out_rows)


def _ec_moe(x, g, w_router, w_gate, w_up, w_down, final_g):
    t, d = x.shape
    e = w_router.shape[1]
    cap = max(1, min(t, CAPACITY_FACTOR * t // e))
    hn, probs_t = _router(x, g, w_router)
    posm, posx = _select(probs_t, cap)
    tile16 = lambda c: (c // BF16_SUBLANE_TILE) * BF16_SUBLANE_TILE
    gather_starts = tile16(posx[:, ::GATHER_BLK])
    cum = posx[:, ::COMB_BLK]
    cnt = jnp.concatenate([cum[:, 1:], jnp.full((e, 1), cap, I32)], axis=1) - cum
    comb_starts = tile16(cum)
    nchunks = (cum - comb_starts + cnt + COMB_WIN - 1) // COMB_WIN
    out_rows = _expert_ffn(hn, posm, gather_starts, w_gate, w_up, w_down, cap)
    return _combine(x, posm.T, probs_t.T, out_rows.reshape(-1, d), comb_starts, nchunks, final_g)


def _encoder(x3, p):
    bt, l, d = x3.shape
    t = bt * l
    x = x3.reshape(t, d)
    depth = p["norm_mix_g"].shape[0]
    for i in range(depth):
        j = i // 2
        if i % 2 == 0:
            u = _norm_glu(x, p["norm_mix_g"][i], p["conf_w_pw1"][j], p["conf_b_pw1"][j])
            x = _conf_tail(u.reshape(bt, l, d), x.reshape(bt, l, d), p["conf_w_dw"][j], p["conf_b_dw"][j],
                           p["conf_ln_g"][j], p["conf_ln_b"][j], p["conf_w_pw2"][j],
                           p["conf_b_pw2"][j]).reshape(t, d)
        else:
            u3 = _norm_proj(x, p["norm_mix_g"][i], p["hy_w_in"][j], p["hy_b_in"][j], tn=d)
            vg, x0 = _short_conv(u3.reshape(bt, l, 3 * d), p["hy_w_short"][j], p["hy_b_short"][j])
            kspec = _hyena_filter_spectrum(l, p["hy_w_f0"][j], p["hy_b_f0"][j], p["hy_w_fin"][j],
                                           p["hy_b_fin"][j], p["hy_w_fout"][j], p["hy_sin_freq"][j])
            y = _long_conv(vg, kspec)
            x = _hyena_tail(y.reshape(t, d), vg.reshape(t, d), x0.reshape(t, d), p["hy_fft_bias"][j],
                            p["hy_w_out"][j], p["hy_b_out"][j], x)
        final_g = p["final_norm_g"] if i == depth - 1 else None
        x = _ec_moe(x, p["norm_ffn_g"][i], p["moe_w_router"][i], p["moe_w_gate"][i], p["moe_w_up"][i],
                    p["moe_w_down"][i], final_g)
    return x.reshape(bt, l, d)


_BF16_WEIGHTS = ("conf_w_pw1", "conf_w_pw2", "hy_w_in", "hy_w_out", "moe_w_gate", "moe_w_up", "moe_w_down")


def kernel(x_prompt, x_sample, norm_mix_g, norm_ffn_g, final_norm_g, conf_w_pw1, conf_b_pw1, conf_w_dw, conf_b_dw, conf_ln_g, conf_ln_b, conf_w_pw2, conf_b_pw2, hy_w_in, hy_b_in, hy_w_short, hy_b_short, hy_w_f0, hy_b_f0, hy_w_fin, hy_b_fin, hy_w_fout, hy_sin_freq, hy_fft_bias, hy_w_out, hy_b_out, moe_w_router, moe_w_gate, moe_w_up, moe_w_down):
    p = dict(norm_mix_g=norm_mix_g, norm_ffn_g=norm_ffn_g, final_norm_g=final_norm_g,
             conf_w_pw1=conf_w_pw1, conf_b_pw1=conf_b_pw1, conf_w_dw=conf_w_dw, conf_b_dw=conf_b_dw,
             conf_ln_g=conf_ln_g, conf_ln_b=conf_ln_b, conf_w_pw2=conf_w_pw2, conf_b_pw2=conf_b_pw2,
             hy_w_in=hy_w_in, hy_b_in=hy_b_in, hy_w_short=hy_w_short, hy_b_short=hy_b_short,
             hy_w_f0=hy_w_f0, hy_b_f0=hy_b_f0, hy_w_fin=hy_w_fin, hy_b_fin=hy_b_fin, hy_w_fout=hy_w_fout,
             hy_sin_freq=hy_sin_freq, hy_fft_bias=hy_fft_bias, hy_w_out=hy_w_out, hy_b_out=hy_b_out,
             moe_w_router=moe_w_router, moe_w_gate=moe_w_gate, moe_w_up=moe_w_up, moe_w_down=moe_w_down)
    for name in _BF16_WEIGHTS:
        p[name] = p[name].astype(BF16)
    return (_encoder(x_prompt, p), _encoder(x_sample, p))
```

```python
import functools
import math

import jax
import jax.numpy as jnp
from jax import lax
from jax.experimental import pallas as pl
from jax.experimental.pallas import tpu as pltpu

F32 = jnp.float32
BF16 = jnp.bfloat16
I32 = jnp.int32

RMS_EPS = 1e-6
LN_EPS = 1e-5
CONV_WIDTH = 31
SHORT_WIDTH = 3
EMB_DIM = 33
N_INNER_MLPS = 2
FAST_DECAY_PCT = 0.3
SLOW_DECAY_PCT = 1.5
DECAY_TARGET = 1e-2
N_EXPERTS = 16
CAPACITY_FACTOR = 2

LANE = 128
BF16_SUBLANE_TILE = 16
MXU_DIM = 256
VMEM_LIMIT = 56 * 1024 * 1024

ROW_TILE = 512
DFT_B = 128
GATHER_BLK = 256
GATHER_WIN = GATHER_BLK + BF16_SUBLANE_TILE
COMB_BLK = 512
COMB_WIN = 128
FFN_ROWS = 512


def _cparams(sem):
    return pltpu.CompilerParams(dimension_semantics=sem, vmem_limit_bytes=VMEM_LIMIT)


def _rms(x, g):
    return x * lax.rsqrt(jnp.mean(x * x, axis=-1, keepdims=True) + RMS_EPS) * g


def _norm_proj_kernel(x_ref, g_ref, w_ref, b_ref, o_ref):
    h = _rms(x_ref[...], g_ref[...]).astype(BF16)
    o_ref[...] = jnp.dot(h, w_ref[...], preferred_element_type=F32) + b_ref[...]


def _norm_glu_kernel(x_ref, g_ref, wa_ref, wg_ref, ba_ref, bg_ref, o_ref):
    h = _rms(x_ref[...], g_ref[...]).astype(BF16)
    a = jnp.dot(h, wa_ref[...], preferred_element_type=F32) + ba_ref[...]
    g = jnp.dot(h, wg_ref[...], preferred_element_type=F32) + bg_ref[...]
    o_ref[...] = a * jax.nn.sigmoid(g)


def _norm_proj(x, g, w, b, tn):
    t, d = x.shape
    n = w.shape[1]
    return pl.pallas_call(
        _norm_proj_kernel,
        grid=(t // ROW_TILE, n // tn),
        in_specs=[
            pl.BlockSpec((ROW_TILE, d), lambda i, j: (i, 0)),
            pl.BlockSpec((1, d), lambda i, j: (0, 0)),
            pl.BlockSpec((d, tn), lambda i, j: (0, j)),
            pl.BlockSpec((1, tn), lambda i, j: (0, j)),
        ],
        out_specs=pl.BlockSpec((ROW_TILE, tn), lambda i, j: (i, j)),
        out_shape=jax.ShapeDtypeStruct((t, n), F32),
        compiler_params=_cparams(("parallel", "parallel")),
        name="norm_proj",
    )(x, g.reshape(1, d), w, b.reshape(1, n))


def _norm_glu(x, g, w, b):
    t, d = x.shape
    n = w.shape[1] // 2
    return pl.pallas_call(
        _norm_glu_kernel,
        grid=(t // ROW_TILE,),
        in_specs=[
            pl.BlockSpec((ROW_TILE, d), lambda i: (i, 0)),
            pl.BlockSpec((1, d), lambda i: (0, 0)),
            pl.BlockSpec((d, n), lambda i: (0, 0)),
            pl.BlockSpec((d, n), lambda i: (0, 1)),
            pl.BlockSpec((1, n), lambda i: (0, 0)),
            pl.BlockSpec((1, n), lambda i: (0, 1)),
        ],
        out_specs=pl.BlockSpec((ROW_TILE, n), lambda i: (i, 0)),
        out_shape=jax.ShapeDtypeStruct((t, n), F32),
        compiler_params=_cparams(("parallel",)),
        name="norm_glu",
    )(x, g.reshape(1, d), w, w, b.reshape(1, 2 * n), b.reshape(1, 2 * n))


HALO = 16


def _conf_tail_kernel(u_ref, up_ref, un_ref, x_ref, wdw_ref, bdw_ref, lng_ref, lnb_ref,
                      w2_ref, b2_ref, o_ref, ext_ref, *, tm, nblk):
    i = pl.program_id(1)
    prev = up_ref[...]
    nxt = un_ref[...]
    ext_ref[pl.ds(0, HALO), :] = jnp.where(i > 0, prev, jnp.zeros_like(prev))
    ext_ref[pl.ds(HALO, tm), :] = u_ref[...]
    ext_ref[pl.ds(HALO + tm, HALO), :] = jnp.where(i < nblk - 1, nxt, jnp.zeros_like(nxt))
    off = HALO - CONV_WIDTH // 2
    acc = ext_ref[pl.ds(off, tm), :] * wdw_ref[pl.ds(0, 1), :] + bdw_ref[...]
    for k in range(1, CONV_WIDTH):
        acc = acc + ext_ref[pl.ds(off + k, tm), :] * wdw_ref[pl.ds(k, 1), :]
    mu = jnp.mean(acc, axis=-1, keepdims=True)
    xc = acc - mu
    var = jnp.mean(xc * xc, axis=-1, keepdims=True)
    y = xc * lax.rsqrt(var + LN_EPS) * lng_ref[...] + lnb_ref[...]
    y = (y * jax.nn.sigmoid(y)).astype(BF16)
    o_ref[...] = jnp.dot(y, w2_ref[...], preferred_element_type=F32) + b2_ref[...] + x_ref[...]


def _conf_tail(u, x, w_dw, b_dw, ln_g, ln_b, w2, b2):
    bt, l, d = u.shape
    tm = ROW_TILE
    nblk = l // tm
    hb = tm // HALO
    wpad = jnp.zeros((32, d), F32).at[:CONV_WIDTH].set(w_dw)
    row = lambda a: a.reshape(1, d)
    return pl.pallas_call(
        functools.partial(_conf_tail_kernel, tm=tm, nblk=nblk),
        grid=(bt, nblk),
        in_specs=[
            pl.BlockSpec((None, tm, d), lambda b, i: (b, i, 0)),
            pl.BlockSpec((None, HALO, d), lambda b, i: (b, jnp.maximum(i * hb - 1, 0), 0)),
            pl.BlockSpec((None, HALO, d), lambda b, i: (b, jnp.minimum((i + 1) * hb, nblk * hb - 1), 0)),
            pl.BlockSpec((None, tm, d), lambda b, i: (b, i, 0)),
            pl.BlockSpec((32, d), lambda b, i: (0, 0)),
            pl.BlockSpec((1, d), lambda b, i: (0, 0)),
            pl.BlockSpec((1, d), lambda b, i: (0, 0)),
            pl.BlockSpec((1, d), lambda b, i: (0, 0)),
            pl.BlockSpec((d, d), lambda b, i: (0, 0)),
            pl.BlockSpec((1, d), lambda b, i: (0, 0)),
        ],
        out_specs=pl.BlockSpec((None, tm, d), lambda b, i: (b, i, 0)),
        out_shape=jax.ShapeDtypeStruct((bt, l, d), F32),
        scratch_shapes=[pltpu.VMEM((tm + 2 * HALO, d), F32)],
        compiler_params=_cparams(("parallel", "parallel")),
        name="conf_tail",
    )(u, u, u, x, wpad, row(b_dw), row(ln_g), row(ln_b), w2, row(b2))


SHALO = 8


def _short_conv_kernel(u_ref, up_ref, un_ref, w_ref, b_ref, vg_ref, vgm_ref, x0_ref, ext_ref, *, tm, nblk, d):
    i = pl.program_id(1)
    prev = up_ref[...]
    nxt = un_ref[...]
    ext_ref[pl.ds(0, SHALO), :] = jnp.where(i > 0, prev, jnp.zeros_like(prev))
    ext_ref[pl.ds(SHALO, tm), :] = u_ref[...]
    ext_ref[pl.ds(SHALO + tm, SHALO), :] = jnp.where(i < nblk - 1, nxt, jnp.zeros_like(nxt))

    def conv(c):
        cols = pl.ds(c * d, d)
        acc = b_ref[:, cols]
        for k in range(SHORT_WIDTH):
            acc = acc + ext_ref[pl.ds(SHALO - 1 + k, tm), cols] * w_ref[pl.ds(k, 1), cols]
        return acc

    x0_ref[...] = conv(0)
    vg = conv(2) * conv(1)
    vg_ref[...] = vg
    _to_group_major(vg, vgm_ref, tm)


DFT_G = DFT_B // 8


def _to_group_major(val, ref, tm):
    for ai in range(tm // DFT_B):
        for g in range(DFT_G):
            ref[g, pl.ds(ai * 8, 8), :] = val[ai * DFT_B + g * 8: ai * DFT_B + g * 8 + 8, :]


def _from_group_major(ref, tm):
    return jnp.concatenate([ref[g, pl.ds(ai * 8, 8), :] for ai in range(tm // DFT_B) for g in range(DFT_G)],
                           axis=0)


def _short_conv(u3, w_short, b_short):
    bt, l, d3 = u3.shape
    d = d3 // 3
    tm = ROW_TILE
    nblk = l // tm
    hb = tm // SHALO
    gm_rows = 8 * tm // DFT_B
    wpad = jnp.zeros((8, d3), F32).at[:SHORT_WIDTH].set(w_short)
    out = jax.ShapeDtypeStruct((bt, l, d), F32)
    return pl.pallas_call(
        functools.partial(_short_conv_kernel, tm=tm, nblk=nblk, d=d),
        grid=(bt, nblk),
        in_specs=[
            pl.BlockSpec((None, tm, d3), lambda b, i: (b, i, 0)),
            pl.BlockSpec((None, SHALO, d3), lambda b, i: (b, jnp.maximum(i * hb - 1, 0), 0)),
            pl.BlockSpec((None, SHALO, d3), lambda b, i: (b, jnp.minimum((i + 1) * hb, nblk * hb - 1), 0)),
            pl.BlockSpec((8, d3), lambda b, i: (0, 0)),
            pl.BlockSpec((1, d3), lambda b, i: (0, 0)),
        ],
        out_specs=[
            pl.BlockSpec((None, tm, d), lambda b, i: (b, i, 0)),
            pl.BlockSpec((None, DFT_G, gm_rows, d), lambda b, i: (b, 0, i, 0)),
            pl.BlockSpec((None, tm, d), lambda b, i: (b, i, 0)),
        ],
        out_shape=[out, jax.ShapeDtypeStruct((bt, DFT_G, 8 * l // DFT_B, d), F32), out],
        scratch_shapes=[pltpu.VMEM((tm + 2 * SHALO, d3), F32)],
        compiler_params=_cparams(("parallel", "parallel")),
        name="short_conv",
    )(u3, u3, u3, wpad, b_short.reshape(1, d3))


_HI16 = -65536


def _pack_complex(re, im):
    hi = lax.bitcast_convert_type(re.astype(BF16).astype(F32), I32)
    lo = lax.bitcast_convert_type(im.astype(BF16).astype(F32), I32)
    return hi | lax.shift_right_logical(lo, jnp.int32(16))


def _unpack_complex(w):
    re = lax.bitcast_convert_type(w & jnp.int32(_HI16), F32)
    im = lax.bitcast_convert_type(lax.shift_left(w, jnp.int32(16)), F32)
    return re.astype(BF16), im.astype(BF16)


def _dft1_kernel(v_ref, fa_ref, twr_ref, twi_ref, x_ref, *, a):
    r = jnp.dot(fa_ref[...], v_ref[...].astype(BF16), preferred_element_type=F32)
    re, im = r[:8 * a], r[8 * a:]
    twr, twi = twr_ref[...], twi_ref[...]
    x_ref[...] = _pack_complex(re * twr - im * twi, re * twi + im * twr)


def _dft2_kernel(x_ref, mf_ref, mi_ref, kr_ref, ki_ref, cr_ref, ci_ref, z_ref, *, at, b):
    d = x_ref.shape[-1]
    for j in range(at):
        xr, xi = _unpack_complex(x_ref[:, j].reshape(b, d))
        v = jnp.dot(mf_ref[...], jnp.concatenate([xr, xi], axis=0), preferred_element_type=F32)
        vr, vi = v[:b], v[b:]
        kr, ki = kr_ref[j], ki_ref[j]
        yr = (vr * kr - vi * ki).astype(BF16)
        yi = (vr * ki + vi * kr).astype(BF16)
        z = jnp.dot(mi_ref[...], jnp.concatenate([yr, yi], axis=0), preferred_element_type=F32)
        zr, zi = z[:b], z[b:]
        cr, ci = cr_ref[j], ci_ref[j]
        z_ref[:, j] = _pack_complex(zr * cr - zi * ci, zr * ci + zi * cr).reshape(DFT_G, 8, d)


def _dft3_kernel(z_ref, fi_ref, y_ref):
    zr, zi = _unpack_complex(z_ref[...])
    y_ref[...] = jnp.dot(fi_ref[...], jnp.concatenate([zr, zi], axis=0), preferred_element_type=F32)


def _long_conv(vgm, kspec):
    bt, _, rows, d = vgm.shape
    l = rows * DFT_B // 8
    n = 2 * l
    b = DFT_B
    a = n // b
    ah = a // 2
    two_pi = 2.0 * math.pi

    f1 = jnp.arange(a, dtype=F32)[:, None]
    aa = jnp.arange(ah, dtype=F32)[None, :]
    ang_a = two_pi * ((f1 * aa) % a) / a
    eye8 = jnp.eye(8, dtype=F32)
    fa = jnp.kron(jnp.concatenate([jnp.cos(ang_a), -jnp.sin(ang_a)], axis=0), eye8).astype(BF16)
    fi = jnp.kron(jnp.concatenate([jnp.cos(ang_a).T, -jnp.sin(ang_a).T], axis=1), eye8).astype(BF16)
    bb = jnp.arange(b, dtype=F32)
    ang_b = two_pi * ((bb[:, None] * bb[None, :]) % b) / b
    cb, sb = jnp.cos(ang_b), jnp.sin(ang_b)
    mf = jnp.concatenate([jnp.concatenate([cb, sb], 1), jnp.concatenate([-sb, cb], 1)], 0).astype(BF16)
    mi = jnp.concatenate([jnp.concatenate([cb, -sb], 1), jnp.concatenate([sb, cb], 1)], 0).astype(BF16)
    ang_t = two_pi * (bb[:, None] * jnp.arange(a, dtype=F32)[None, :]) / n
    by_group = lambda m: m.reshape(DFT_G, 8, a).transpose(0, 2, 1).reshape(DFT_G, 8 * a, 1)
    twr, twi = by_group(jnp.cos(ang_t)), by_group(-jnp.sin(ang_t))
    cr, ci = jnp.cos(ang_t).T[..., None], jnp.sin(ang_t).T[..., None]

    k2 = kspec.reshape(b, a, d).transpose(1, 0, 2)
    kr, ki = jnp.real(k2).astype(F32), jnp.imag(k2).astype(F32)

    x = pl.pallas_call(
        functools.partial(_dft1_kernel, a=a),
        grid=(bt, DFT_G),
        in_specs=[
            pl.BlockSpec((None, None, 8 * ah, d), lambda n_, g: (n_, g, 0, 0)),
            pl.BlockSpec((16 * a, 8 * ah), lambda n_, g: (0, 0)),
            pl.BlockSpec((None, 8 * a, 1), lambda n_, g: (g, 0, 0)),
            pl.BlockSpec((None, 8 * a, 1), lambda n_, g: (g, 0, 0)),
        ],
        out_specs=pl.BlockSpec((None, None, 8 * a, d), lambda n_, g: (n_, g, 0, 0)),
        out_shape=jax.ShapeDtypeStruct((bt, DFT_G, 8 * a, d), I32),
        compiler_params=_cparams(("parallel", "parallel")),
        name="dft_stage1",
    )(vgm, fa, twr, twi)

    at = 4
    z = pl.pallas_call(
        functools.partial(_dft2_kernel, at=at, b=b),
        grid=(a // at, bt),
        in_specs=[
            pl.BlockSpec((None, DFT_G, at, 8, d), lambda i, n_: (n_, 0, i, 0, 0)),
            pl.BlockSpec((2 * b, 2 * b), lambda i, n_: (0, 0)),
            pl.BlockSpec((2 * b, 2 * b), lambda i, n_: (0, 0)),
            pl.BlockSpec((at, b, d), lambda i, n_: (i, 0, 0)),
            pl.BlockSpec((at, b, d), lambda i, n_: (i, 0, 0)),
            pl.BlockSpec((at, b, 1), lambda i, n_: (i, 0, 0)),
            pl.BlockSpec((at, b, 1), lambda i, n_: (i, 0, 0)),
        ],
        out_specs=pl.BlockSpec((None, DFT_G, at, 8, d), lambda i, n_: (n_, 0, i, 0, 0)),
        out_shape=jax.ShapeDtypeStruct((bt, DFT_G, a, 8, d), I32),
        compiler_params=_cparams(("parallel", "parallel")),
        name="dft_stage2",
    )(x.reshape(bt, DFT_G, a, 8, d), mf, mi, kr, ki, cr, ci)

    return pl.pallas_call(
        _dft3_kernel,
        grid=(bt, DFT_G),
        in_specs=[
            pl.BlockSpec((None, None, 8 * a, d), lambda n_, g: (n_, g, 0, 0)),
            pl.BlockSpec((8 * ah, 16 * a), lambda n_, g: (0, 0)),
        ],
        out_specs=pl.BlockSpec((None, None, 8 * ah, d), lambda n_, g: (n_, g, 0, 0)),
        out_shape=jax.ShapeDtypeStruct((bt, DFT_G, 8 * ah, d), F32),
        compiler_params=_cparams(("parallel", "parallel")),
        name="dft_stage3",
    )(z.reshape(bt, DFT_G, 8 * a, d), fi)


def _hyena_filter_spectrum(l, w_f0, b_f0, w_fin, b_fin, w_fout, sin_freq):
    hp = lax.Precision.HIGHEST
    d = w_fout.shape[-1] // 2
    t = jnp.linspace(0.0, 1.0, l, dtype=F32)[:, None]
    bands = (EMB_DIM - 1) // 2
    f = jnp.linspace(1e-4, bands - 1, bands, dtype=F32)[None, :]
    w = 2.0 * math.pi * jnp.arange(l, dtype=F32)[:, None] / l
    z = jnp.concatenate([t, jnp.cos(f * w), -jnp.sin(f * w)], axis=-1)
    hf = jnp.sin(sin_freq * (jnp.dot(z, w_f0, precision=hp) + b_f0))
    for i in range(N_INNER_MLPS):
        hf = jnp.sin(sin_freq * (jnp.dot(hf, w_fin[i], precision=hp) + b_fin[i]))
    hf = jnp.dot(hf, w_fout, precision=hp)
    max_decay = math.log(DECAY_TARGET) / FAST_DECAY_PCT
    min_decay = math.log(DECAY_TARGET) / SLOW_DECAY_PCT
    deltas = jnp.linspace(min_decay, max_decay, d, dtype=F32)[None, :]
    decay = jnp.exp(-t * jnp.abs(deltas))
    h_fwd = hf[:, :d] * decay
    h_bwd = hf[:, d:] * decay
    k = jnp.concatenate([h_fwd, jnp.zeros((1, d), F32), h_bwd[:0:-1]], axis=0)
    k = k / jnp.sum(jnp.abs(k), axis=0, keepdims=True)
    return jnp.fft.fft(k, axis=0) / (2 * l)


def _hyena_tail_kernel(y_ref, v_ref, x0_ref, fb_ref, w_ref, b_ref, x_ref, o_ref, *, tm):
    y = _from_group_major(y_ref, tm)
    g = ((y + v_ref[...] * fb_ref[...]) * x0_ref[...]).astype(BF16)
    o_ref[...] = jnp.dot(g, w_ref[...], preferred_element_type=F32) + b_ref[...] + x_ref[...]


def _hyena_tail(ygm, v, x0, fft_bias, w_out, b_out, x):
    bt, l, d = x.shape
    tm = ROW_TILE
    tile = pl.BlockSpec((None, tm, d), lambda b, i: (b, i, 0))
    rowv = pl.BlockSpec((1, d), lambda b, i: (0, 0))
    return pl.pallas_call(
        functools.partial(_hyena_tail_kernel, tm=tm),
        grid=(bt, l // tm),
        in_specs=[pl.BlockSpec((None, DFT_G, 8 * tm // DFT_B, d), lambda b, i: (b, 0, i, 0)),
                  tile, tile, rowv, pl.BlockSpec((d, d), lambda b, i: (0, 0)), rowv, tile],
        out_specs=tile,
        out_shape=jax.ShapeDtypeStruct((bt, l, d), F32),
        compiler_params=_cparams(("parallel", "parallel")),
        name="hyena_tail",
    )(ygm, v, x0, fft_bias.reshape(1, d), w_out, b_out.reshape(1, d), x)


def _router_kernel(x_ref, g_ref, wt_ref, hn_ref, pt_ref):
    h = _rms(x_ref[...], g_ref[...])
    hn_ref[...] = h.astype(BF16)
    lt = lax.dot_general(wt_ref[...], h, (((1,), (1,)), ((), ())), precision=lax.Precision.HIGHEST,
                         preferred_element_type=F32)
    et = jnp.exp(lt - jnp.max(lt, axis=0, keepdims=True))
    pt_ref[...] = et / jnp.sum(et, axis=0, keepdims=True)


def _router(x, g, w_router):
    t, d = x.shape
    e = w_router.shape[1]
    return pl.pallas_call(
        _router_kernel,
        grid=(t // ROW_TILE,),
        in_specs=[
            pl.BlockSpec((ROW_TILE, d), lambda i: (i, 0)),
            pl.BlockSpec((1, d), lambda i: (0, 0)),
            pl.BlockSpec((e, d), lambda i: (0, 0)),
        ],
        out_specs=[
            pl.BlockSpec((ROW_TILE, d), lambda i: (i, 0)),
            pl.BlockSpec((e, ROW_TILE), lambda i: (0, i)),
        ],
        out_shape=[
            jax.ShapeDtypeStruct((t, d), BF16),
            jax.ShapeDtypeStruct((e, t), F32),
        ],
        compiler_params=_cparams(("parallel",)),
        name="router",
    )(x, g.reshape(1, d), w_router.T)


CUM_BLK = 256


def _select_kernel(p_ref, posm_ref, posx_ref, mask_ref, *, cap, t):
    keys = pltpu.bitcast(p_ref[...], I32)
    e = keys.shape[0]

    def vbit(i, tau):
        cand = tau | jnp.left_shift(jnp.int32(1), 30 - i)
        cnt = jnp.sum((keys >= cand).astype(I32), axis=1, keepdims=True)
        return jnp.where(cnt >= cap, cand, tau)

    tau = lax.fori_loop(0, 31, vbit, jnp.zeros((e, 1), I32))
    gt = keys > tau
    eq = keys == tau
    quota = cap - jnp.sum(gt.astype(I32), axis=1, keepdims=True)
    idx = lax.broadcasted_iota(I32, keys.shape, 1)
    nbits = max(1, (t - 1).bit_length())

    def ibit(i, jj):
        cand = jj | jnp.left_shift(jnp.int32(1), nbits - 1 - i)
        cnt = jnp.sum((eq & (idx < cand)).astype(I32), axis=1, keepdims=True)
        return jnp.where(cnt < quota, cand, jj)

    jlast = lax.fori_loop(0, nbits, ibit, jnp.zeros((e, 1), I32))
    mask = gt | (eq & (idx <= jlast))
    mask_ref[...] = mask.astype(F32)

    r = lax.broadcasted_iota(I32, (CUM_BLK, CUM_BLK), 0)
    c = lax.broadcasted_iota(I32, (CUM_BLK, CUM_BLK), 1)
    tri = (r <= c).astype(BF16)
    run = jnp.zeros((e, 1), F32)
    for blk in range(t // CUM_BLK):
        cols = pl.ds(blk * CUM_BLK, CUM_BLK)
        m = mask_ref[:, cols]
        incl = jnp.dot(m.astype(BF16), tri, preferred_element_type=F32)
        pos = (incl - m + run).astype(I32)
        posx_ref[:, cols] = pos
        posm_ref[:, cols] = jnp.where(m > 0.0, pos, -1)
        run = run + incl[:, CUM_BLK - 1:CUM_BLK]


def _select(probs_t, cap):
    e, t = probs_t.shape
    full = pl.BlockSpec((e, t), lambda: (0, 0))
    return pl.pallas_call(
        functools.partial(_select_kernel, cap=cap, t=t),
        in_specs=[full],
        out_specs=[full, full],
        out_shape=[jax.ShapeDtypeStruct((e, t), I32), jax.ShapeDtypeStruct((e, t), I32)],
        scratch_shapes=[pltpu.VMEM((e, t), F32)],
        compiler_params=pltpu.CompilerParams(vmem_limit_bytes=VMEM_LIMIT),
        name="select",
    )(probs_t)


def _gather_kernel(st_ref, hn_ref, posm_ref, o_ref, *, sub):
    e = pl.program_id(0)
    k = pl.program_id(1)

    @pl.when(k == 0)
    def _():
        o_ref[...] = jnp.zeros_like(o_ref)

    rows = lax.broadcasted_iota(I32, (GATHER_WIN, GATHER_BLK), 0)
    for j in range(sub):
        base = pl.multiple_of(st_ref[e, k * sub + j], BF16_SUBLANE_TILE)
        rel = posm_ref[:, pl.ds(j * GATHER_BLK, GATHER_BLK)] - base
        onehot = (rows == rel).astype(BF16)
        slab = jnp.dot(onehot, hn_ref[pl.ds(j * GATHER_BLK, GATHER_BLK), :],
                       preferred_element_type=F32)
        win = pl.ds(base, GATHER_WIN)
        o_ref[win, :] = (o_ref[win, :].astype(F32) + slab).astype(BF16)


def _gather_tokens(hn, posm, starts, cap):
    t, d = hn.shape
    e = posm.shape[0]
    sub = 4
    nb = t // (GATHER_BLK * sub)
    cp = cap + GATHER_WIN
    return pl.pallas_call(
        functools.partial(_gather_kernel, sub=sub),
        grid_spec=pltpu.PrefetchScalarGridSpec(
            num_scalar_prefetch=1,
            grid=(e, nb),
            in_specs=[
                pl.BlockSpec((GATHER_BLK * sub, d), lambda ei, k, st: (k, 0)),
                pl.BlockSpec((None, 1, GATHER_BLK * sub), lambda ei, k, st: (ei, 0, k)),
            ],
            out_specs=pl.BlockSpec((None, cp, d), lambda ei, k, st: (ei, 0, 0)),
        ),
        out_shape=jax.ShapeDtypeStruct((e, cp, d), BF16),
        compiler_params=_cparams(("arbitrary", "arbitrary")),
        name="gather_tokens",
    )(starts, hn, posm.reshape(e, 1, t))


def _ffn_kernel(x_ref, wg_ref, wu_ref, wd_ref, o_ref, *, nrt, splits):
    r = pl.program_id(1)

    @pl.when(r < nrt)
    def _():
        x = x_ref[...]
        dn = None
        for c0, c1 in splits:
            g = jnp.dot(x, wg_ref[:, c0:c1], preferred_element_type=F32)
            u = jnp.dot(x, wu_ref[:, c0:c1], preferred_element_type=F32)
            h = (g * jax.nn.sigmoid(g) * u).astype(BF16)
            part = jnp.dot(h, wd_ref[c0:c1, :], preferred_element_type=F32)
            dn = part if dn is None else dn + part
        o_ref[...] = dn.astype(BF16)

    @pl.when(r == nrt)
    def _():
        o_ref[...] = jnp.zeros_like(o_ref)


def _expert_ffn(xs, w_gate, w_up, w_down, cap):
    e, _, d = xs.shape
    f = w_gate.shape[2]
    nrt = cap // FFN_ROWS
    half = (f // MXU_DIM + 1) // 2 * MXU_DIM
    splits = ((0, half), (half, f))
    row = lambda r: jnp.minimum(r, nrt - 1)
    return pl.pallas_call(
        functools.partial(_ffn_kernel, nrt=nrt, splits=splits),
        grid=(e, nrt + 1),
        in_specs=[
            pl.BlockSpec((None, FFN_ROWS, d), lambda ei, r: (ei, row(r), 0)),
            pl.BlockSpec((None, d, f), lambda ei, r: (ei, 0, 0)),
            pl.BlockSpec((None, d, f), lambda ei, r: (ei, 0, 0)),
            pl.BlockSpec((None, f, d), lambda ei, r: (ei, 0, 0)),
        ],
        out_specs=pl.BlockSpec((None, FFN_ROWS, d), lambda ei, r: (ei, r, 0)),
        out_shape=jax.ShapeDtypeStruct((e, cap + FFN_ROWS, d), BF16),
        compiler_params=_cparams(("parallel", "arbitrary")),
        name="expert_ffn",
    )(xs, w_gate, w_up, w_down)


def _combine_kernel(st_ref, nch_ref, x_ref, posm_ref, p_ref, g_ref, out_hbm, o_ref,
                    stack, lhs, xbuf, sem, xsem, *, ne, cp, final_norm):
    b = pl.program_id(0)
    nsteps = pl.num_programs(0)

    def window_copies(step, slot):
        for ei in range(ne):
            start = pl.multiple_of(ei * cp + st_ref[ei, step], BF16_SUBLANE_TILE)
            for part in range(2):
                dst = stack.at[slot, pl.ds((2 * ei + part) * COMB_WIN, COMB_WIN), :]
                yield pltpu.make_async_copy(out_hbm.at[pl.ds(start, COMB_WIN), :], dst, sem.at[slot])

    @pl.when(b == 0)
    def _():
        for c in window_copies(0, 0):
            c.start()

    @pl.when(b + 1 < nsteps)
    def _():
        for c in window_copies(b + 1, (b + 1) % 2):
            c.start()

    cols = lax.broadcasted_iota(I32, (COMB_BLK, COMB_WIN), 1)

    def weighted_onehot(ei, chunk):
        rel = posm_ref[:, ei:ei + 1] - st_ref[ei, b] - chunk * COMB_WIN
        return jnp.where(cols == rel, p_ref[:, ei:ei + 1], 0.0)

    for ei in range(ne):
        v = weighted_onehot(ei, 0)
        hi = v.astype(BF16)
        lhs[:, pl.ds(2 * ei * COMB_WIN, COMB_WIN)] = hi
        lhs[:, pl.ds((2 * ei + 1) * COMB_WIN, COMB_WIN)] = (v - hi.astype(F32)).astype(BF16)

    slot = b % 2
    for c in window_copies(b, slot):
        c.wait()
    o_ref[...] = x_ref[...] + jnp.dot(lhs[...], stack[slot], preferred_element_type=F32)

    for ei in range(ne):
        def extra(chunk, carry):
            start = pl.multiple_of(ei * cp + st_ref[ei, b] + chunk * COMB_WIN, BF16_SUBLANE_TILE)
            cpy = pltpu.make_async_copy(out_hbm.at[pl.ds(start, COMB_WIN), :], xbuf, xsem)
            cpy.start()
            cpy.wait()
            v = weighted_onehot(ei, chunk)
            hi = v.astype(BF16)
            lo = (v - hi.astype(F32)).astype(BF16)
            rows = xbuf[...]
            o_ref[...] = (o_ref[...] + jnp.dot(hi, rows, preferred_element_type=F32)
                          + jnp.dot(lo, rows, preferred_element_type=F32))
            return carry

        lax.fori_loop(1, nch_ref[ei, b], extra, 0)

    if final_norm:
        o_ref[...] = _rms(o_ref[...], g_ref[...])


def _combine(x, posm_tok, probs, out_rows, starts, nchunks, final_g):
    t, d = x.shape
    e = posm_tok.shape[1]
    cp = out_rows.shape[0] // e
    final_norm = final_g is not None
    g = final_g if final_norm else jnp.ones((d,), F32)
    return pl.pallas_call(
        functools.partial(_combine_kernel, ne=e, cp=cp, final_norm=final_norm),
        grid_spec=pltpu.PrefetchScalarGridSpec(
            num_scalar_prefetch=2,
            grid=(t // COMB_BLK,),
            in_specs=[
                pl.BlockSpec((COMB_BLK, d), lambda i, st, nc: (i, 0)),
                pl.BlockSpec((COMB_BLK, e), lambda i, st, nc: (i, 0)),
                pl.BlockSpec((COMB_BLK, e), lambda i, st, nc: (i, 0)),
                pl.BlockSpec((1, d), lambda i, st, nc: (0, 0)),
                pl.BlockSpec(memory_space=pl.ANY),
            ],
            out_specs=pl.BlockSpec((COMB_BLK, d), lambda i, st, nc: (i, 0)),
            scratch_shapes=[
                pltpu.VMEM((2, 2 * e * COMB_WIN, d), BF16),
                pltpu.VMEM((COMB_BLK, 2 * e * COMB_WIN), BF16),
                pltpu.VMEM((COMB_WIN, d), BF16),
                pltpu.SemaphoreType.DMA((2,)),
                pltpu.SemaphoreType.DMA,
            ],
        ),
        out_shape=jax.ShapeDtypeStruct((t, d), F32),
        compiler_params=_cparams(("arbitrary",)),
        name="combine",
    )(starts, nchunks, x, posm_tok, probs, g.reshape(1, d), out_rows)


def _ec_moe(x, g, w_router, w_gate, w_up, w_down, final_g):
    t, d = x.shape
    e = w_router.shape[1]
    cap = max(1, min(t, CAPACITY_FACTOR * t // e))
    hn, probs_t = _router(x, g, w_router)
    posm, posx = _select(probs_t, cap)
    tile16 = lambda c: (c // BF16_SUBLANE_TILE) * BF16_SUBLANE_TILE
    gather_starts = tile16(posx[:, ::GATHER_BLK])
    cum = posx[:, ::COMB_BLK]
    cnt = jnp.concatenate([cum[:, 1:], jnp.full((e, 1), cap, I32)], axis=1) - cum
    comb_starts = tile16(cum)
    nchunks = (cum - comb_starts + cnt + COMB_WIN - 1) // COMB_WIN
    xs = _gather_tokens(hn, posm, gather_starts, cap)
    out_rows = _expert_ffn(xs, w_gate, w_up, w_down, cap)
    return _combine(x, posm.T, probs_t.T, out_rows.reshape(-1, d), comb_starts, nchunks, final_g)


def _encoder(x3, p):
    bt, l, d = x3.shape
    t = bt * l
    x = x3.reshape(t, d)
    depth = p["norm_mix_g"].shape[0]
    for i in range(depth):
        j = i // 2
        if i % 2 == 0:
            u = _norm_glu(x, p["norm_mix_g"][i], p["conf_w_pw1"][j], p["conf_b_pw1"][j])
            x = _conf_tail(u.reshape(bt, l, d), x.reshape(bt, l, d), p["conf_w_dw"][j], p["conf_b_dw"][j],
                           p["conf_ln_g"][j], p["conf_ln_b"][j], p["conf_w_pw2"][j],
                           p["conf_b_pw2"][j]).reshape(t, d)
        else:
            u3 = _norm_proj(x, p["norm_mix_g"][i], p["hy_w_in"][j], p["hy_b_in"][j], tn=d)
            vg, vgm, x0 = _short_conv(u3.reshape(bt, l, 3 * d), p["hy_w_short"][j], p["hy_b_short"][j])
            kspec = _hyena_filter_spectrum(l, p["hy_w_f0"][j], p["hy_b_f0"][j], p["hy_w_fin"][j],
                                           p["hy_b_fin"][j], p["hy_w_fout"][j], p["hy_sin_freq"][j])
            ygm = _long_conv(vgm, kspec)
            x = _hyena_tail(ygm, vg, x0, p["hy_fft_bias"][j], p["hy_w_out"][j], p["hy_b_out"][j],
                            x.reshape(bt, l, d)).reshape(t, d)
        final_g = p["final_norm_g"] if i == depth - 1 else None
        x = _ec_moe(x, p["norm_ffn_g"][i], p["moe_w_router"][i], p["moe_w_gate"][i], p["moe_w_up"][i],
                    p["moe_w_down"][i], final_g)
    return x.reshape(bt, l, d)


_BF16_WEIGHTS = ("conf_w_pw1", "conf_w_pw2", "hy_w_in", "hy_w_out", "moe_w_gate", "moe_w_up", "moe_w_down")


def kernel(x_prompt, x_sample, norm_mix_g, norm_ffn_g, final_norm_g, conf_w_pw1, conf_b_pw1, conf_w_dw, conf_b_dw, conf_ln_g, conf_ln_b, conf_w_pw2, conf_b_pw2, hy_w_in, hy_b_in, hy_w_short, hy_b_short, hy_w_f0, hy_b_f0, hy_w_fin, hy_b_fin, hy_w_fout, hy_sin_freq, hy_fft_bias, hy_w_out, hy_b_out, moe_w_router, moe_w_gate, moe_w_up, moe_w_down):
    p = dict(norm_mix_g=norm_mix_g, norm_ffn_g=norm_ffn_g, final_norm_g=final_norm_g,
             conf_w_pw1=conf_w_pw1, conf_b_pw1=conf_b_pw1, conf_w_dw=conf_w_dw, conf_b_dw=conf_b_dw,
             conf_ln_g=conf_ln_g, conf_ln_b=conf_ln_b, conf_w_pw2=conf_w_pw2, conf_b_pw2=conf_b_pw2,
             hy_w_in=hy_w_in, hy_b_in=hy_b_in, hy_w_short=hy_w_short, hy_b_short=hy_b_short,
             hy_w_f0=hy_w_f0, hy_b_f0=hy_b_f0, hy_w_fin=hy_w_fin, hy_b_fin=hy_b_fin, hy_w_fout=hy_w_fout,
             hy_sin_freq=hy_sin_freq, hy_fft_bias=hy_fft_bias, hy_w_out=hy_w_out, hy_b_out=hy_b_out,
             moe_w_router=moe_w_router, moe_w_gate=moe_w_gate, moe_w_up=moe_w_up, moe_w_down=moe_w_down)
    for name in _BF16_WEIGHTS:
        p[name] = p[name].astype(BF16)
    return (_encoder(x_prompt, p), _encoder(x_sample, p))
```

```python
import functools
import math

import jax
import jax.numpy as jnp
from jax import lax
from jax.experimental import pallas as pl
from jax.experimental.pallas import tpu as pltpu

F32 = jnp.float32
BF16 = jnp.bfloat16
I32 = jnp.int32

RMS_EPS = 1e-6
LN_EPS = 1e-5
CONV_WIDTH = 31
SHORT_WIDTH = 3
EMB_DIM = 33
N_INNER_MLPS = 2
FAST_DECAY_PCT = 0.3
SLOW_DECAY_PCT = 1.5
DECAY_TARGET = 1e-2
N_EXPERTS = 16
CAPACITY_FACTOR = 2

LANE = 128
BF16_SUBLANE_TILE = 16
MXU_DIM = 256
VMEM_LIMIT = 56 * 1024 * 1024

ROW_TILE = 512
DFT_B = 128
GATHER_BLK = 256
GATHER_WIN = GATHER_BLK + BF16_SUBLANE_TILE
COMB_BLK = 512
COMB_WIN = 128
FFN_ROWS = 512


def _cparams(sem):
    return pltpu.CompilerParams(dimension_semantics=sem, vmem_limit_bytes=VMEM_LIMIT)


def _rms(x, g):
    return x * lax.rsqrt(jnp.mean(x * x, axis=-1, keepdims=True) + RMS_EPS) * g


def _norm_proj_kernel(x_ref, g_ref, w_ref, b_ref, o_ref):
    h = _rms(x_ref[...], g_ref[...]).astype(BF16)
    o_ref[...] = jnp.dot(h, w_ref[...], preferred_element_type=F32) + b_ref[...]


def _norm_glu_kernel(x_ref, g_ref, wa_ref, wg_ref, ba_ref, bg_ref, o_ref):
    h = _rms(x_ref[...], g_ref[...]).astype(BF16)
    a = jnp.dot(h, wa_ref[...], preferred_element_type=F32) + ba_ref[...]
    g = jnp.dot(h, wg_ref[...], preferred_element_type=F32) + bg_ref[...]
    o_ref[...] = a * jax.nn.sigmoid(g)


def _norm_proj(x, g, w, b, tn):
    t, d = x.shape
    n = w.shape[1]
    return pl.pallas_call(
        _norm_proj_kernel,
        grid=(t // ROW_TILE, n // tn),
        in_specs=[
            pl.BlockSpec((ROW_TILE, d), lambda i, j: (i, 0)),
            pl.BlockSpec((1, d), lambda i, j: (0, 0)),
            pl.BlockSpec((d, tn), lambda i, j: (0, j)),
            pl.BlockSpec((1, tn), lambda i, j: (0, j)),
        ],
        out_specs=pl.BlockSpec((ROW_TILE, tn), lambda i, j: (i, j)),
        out_shape=jax.ShapeDtypeStruct((t, n), F32),
        compiler_params=_cparams(("parallel", "parallel")),
        name="norm_proj",
    )(x, g.reshape(1, d), w, b.reshape(1, n))


def _norm_glu(x, g, w, b):
    t, d = x.shape
    n = w.shape[1] // 2
    return pl.pallas_call(
        _norm_glu_kernel,
        grid=(t // ROW_TILE,),
        in_specs=[
            pl.BlockSpec((ROW_TILE, d), lambda i: (i, 0)),
            pl.BlockSpec((1, d), lambda i: (0, 0)),
            pl.BlockSpec((d, n), lambda i: (0, 0)),
            pl.BlockSpec((d, n), lambda i: (0, 1)),
            pl.BlockSpec((1, n), lambda i: (0, 0)),
            pl.BlockSpec((1, n), lambda i: (0, 1)),
        ],
        out_specs=pl.BlockSpec((ROW_TILE, n), lambda i: (i, 0)),
        out_shape=jax.ShapeDtypeStruct((t, n), F32),
        compiler_params=_cparams(("parallel",)),
        name="norm_glu",
    )(x, g.reshape(1, d), w, w, b.reshape(1, 2 * n), b.reshape(1, 2 * n))


HALO = 16


CONV_CHUNK = 16


def _conf_tail_kernel(u_ref, up_ref, un_ref, x_ref, wdw_ref, bdw_ref, lng_ref, lnb_ref,
                      w2_ref, b2_ref, o_ref, ext_ref, acc_ref, *, tm, nblk):
    i = pl.program_id(1)
    d = u_ref.shape[-1]
    sl = d // LANE
    prev = up_ref[...].reshape(HALO, sl, LANE)
    nxt = un_ref[...].reshape(HALO, sl, LANE)
    ext_ref[pl.ds(0, HALO)] = jnp.where(i > 0, prev, jnp.zeros_like(prev))
    ext_ref[pl.ds(HALO, tm)] = u_ref[...].reshape(tm, sl, LANE)
    ext_ref[pl.ds(HALO + tm, HALO)] = jnp.where(i < nblk - 1, nxt, jnp.zeros_like(nxt))
    off = HALO - CONV_WIDTH // 2

    def chunk(c, carry):
        r0 = pl.multiple_of(c * CONV_CHUNK, CONV_CHUNK)
        acc = jnp.broadcast_to(bdw_ref[...], (CONV_CHUNK, sl, LANE))
        for k in range(CONV_WIDTH):
            acc = acc + ext_ref[pl.ds(r0 + off + k, CONV_CHUNK)] * wdw_ref[pl.ds(k, 1)]
        acc_ref[pl.ds(r0, CONV_CHUNK)] = acc
        return carry

    lax.fori_loop(0, tm // CONV_CHUNK, chunk, 0)
    acc = acc_ref[...].reshape(tm, d)
    mu = jnp.mean(acc, axis=-1, keepdims=True)
    xc = acc - mu
    var = jnp.mean(xc * xc, axis=-1, keepdims=True)
    y = xc * lax.rsqrt(var + LN_EPS) * lng_ref[...] + lnb_ref[...]
    y = (y * jax.nn.sigmoid(y)).astype(BF16)
    o_ref[...] = jnp.dot(y, w2_ref[...], preferred_element_type=F32) + b2_ref[...] + x_ref[...]


def _conf_tail(u, x, w_dw, b_dw, ln_g, ln_b, w2, b2):
    bt, l, d = u.shape
    tm = ROW_TILE
    nblk = l // tm
    hb = tm // HALO
    sl = d // LANE
    wpad = jnp.zeros((32, d), F32).at[:CONV_WIDTH].set(w_dw).reshape(32, sl, LANE)
    row = lambda a: a.reshape(1, d)
    return pl.pallas_call(
        functools.partial(_conf_tail_kernel, tm=tm, nblk=nblk),
        grid=(bt, nblk),
        in_specs=[
            pl.BlockSpec((None, tm, d), lambda b, i: (b, i, 0)),
            pl.BlockSpec((None, HALO, d), lambda b, i: (b, jnp.maximum(i * hb - 1, 0), 0)),
            pl.BlockSpec((None, HALO, d), lambda b, i: (b, jnp.minimum((i + 1) * hb, nblk * hb - 1), 0)),
            pl.BlockSpec((None, tm, d), lambda b, i: (b, i, 0)),
            pl.BlockSpec((32, sl, LANE), lambda b, i: (0, 0, 0)),
            pl.BlockSpec((1, sl, LANE), lambda b, i: (0, 0, 0)),
            pl.BlockSpec((1, d), lambda b, i: (0, 0)),
            pl.BlockSpec((1, d), lambda b, i: (0, 0)),
            pl.BlockSpec((d, d), lambda b, i: (0, 0)),
            pl.BlockSpec((1, d), lambda b, i: (0, 0)),
        ],
        out_specs=pl.BlockSpec((None, tm, d), lambda b, i: (b, i, 0)),
        out_shape=jax.ShapeDtypeStruct((bt, l, d), F32),
        scratch_shapes=[pltpu.VMEM((tm + 2 * HALO, sl, LANE), F32), pltpu.VMEM((tm, sl, LANE), F32)],
        compiler_params=_cparams(("parallel", "parallel")),
        name="conf_tail",
    )(u, u, u, x, wpad, b_dw.reshape(1, sl, LANE), row(ln_g), row(ln_b), w2, row(b2))


SHALO = 8


def _short_conv_kernel(u_ref, up_ref, un_ref, w_ref, b_ref, vg_ref, vgm_ref, x0_ref, ext_ref, *, tm, nblk, d):
    i = pl.program_id(1)
    prev = up_ref[...]
    nxt = un_ref[...]
    ext_ref[pl.ds(0, SHALO), :] = jnp.where(i > 0, prev, jnp.zeros_like(prev))
    ext_ref[pl.ds(SHALO, tm), :] = u_ref[...]
    ext_ref[pl.ds(SHALO + tm, SHALO), :] = jnp.where(i < nblk - 1, nxt, jnp.zeros_like(nxt))

    def conv(c):
        cols = pl.ds(c * d, d)
        acc = b_ref[:, cols]
        for k in range(SHORT_WIDTH):
            acc = acc + ext_ref[pl.ds(SHALO - 1 + k, tm), cols] * w_ref[pl.ds(k, 1), cols]
        return acc

    x0_ref[...] = conv(0)
    vg = conv(2) * conv(1)
    vg_ref[...] = vg
    _to_group_major(vg, vgm_ref, tm)


DFT_G = DFT_B // 8


def _to_group_major(val, ref, tm):
    for ai in range(tm // DFT_B):
        for g in range(DFT_G):
            ref[g, pl.ds(ai * 8, 8), :] = val[ai * DFT_B + g * 8: ai * DFT_B + g * 8 + 8, :]


def _from_group_major(ref, tm):
    return jnp.concatenate([ref[g, pl.ds(ai * 8, 8), :] for ai in range(tm // DFT_B) for g in range(DFT_G)],
                           axis=0)


def _short_conv(u3, w_short, b_short):
    bt, l, d3 = u3.shape
    d = d3 // 3
    tm = ROW_TILE
    nblk = l // tm
    hb = tm // SHALO
    gm_rows = 8 * tm // DFT_B
    wpad = jnp.zeros((8, d3), F32).at[:SHORT_WIDTH].set(w_short)
    out = jax.ShapeDtypeStruct((bt, l, d), F32)
    return pl.pallas_call(
        functools.partial(_short_conv_kernel, tm=tm, nblk=nblk, d=d),
        grid=(bt, nblk),
        in_specs=[
            pl.BlockSpec((None, tm, d3), lambda b, i: (b, i, 0)),
            pl.BlockSpec((None, SHALO, d3), lambda b, i: (b, jnp.maximum(i * hb - 1, 0), 0)),
            pl.BlockSpec((None, SHALO, d3), lambda b, i: (b, jnp.minimum((i + 1) * hb, nblk * hb - 1), 0)),
            pl.BlockSpec((8, d3), lambda b, i: (0, 0)),
            pl.BlockSpec((1, d3), lambda b, i: (0, 0)),
        ],
        out_specs=[
            pl.BlockSpec((None, tm, d), lambda b, i: (b, i, 0)),
            pl.BlockSpec((None, DFT_G, gm_rows, d), lambda b, i: (b, 0, i, 0)),
            pl.BlockSpec((None, tm, d), lambda b, i: (b, i, 0)),
        ],
        out_shape=[out, jax.ShapeDtypeStruct((bt, DFT_G, 8 * l // DFT_B, d), F32), out],
        scratch_shapes=[pltpu.VMEM((tm + 2 * SHALO, d3), F32)],
        compiler_params=_cparams(("parallel", "parallel")),
        name="short_conv",
    )(u3, u3, u3, wpad, b_short.reshape(1, d3))


_HI16 = -65536


def _pack_complex(re, im):
    hi = lax.bitcast_convert_type(re.astype(BF16).astype(F32), I32)
    lo = lax.bitcast_convert_type(im.astype(BF16).astype(F32), I32)
    return hi | lax.shift_right_logical(lo, jnp.int32(16))


def _unpack_complex(w):
    re = lax.bitcast_convert_type(w & jnp.int32(_HI16), F32)
    im = lax.bitcast_convert_type(lax.shift_left(w, jnp.int32(16)), F32)
    return re.astype(BF16), im.astype(BF16)


def _dft1_kernel(v_ref, fa_ref, twr_ref, twi_ref, x_ref, *, a):
    r = jnp.dot(fa_ref[...], v_ref[...].astype(BF16), preferred_element_type=F32)
    re, im = r[:8 * a], r[8 * a:]
    twr, twi = twr_ref[...], twi_ref[...]
    x_ref[...] = _pack_complex(re * twr - im * twi, re * twi + im * twr)


def _dft2_kernel(x_ref, mf_ref, mi_ref, kr_ref, ki_ref, cr_ref, ci_ref, z_ref, *, at, b):
    d = x_ref.shape[-1]
    for j in range(at):
        xr, xi = _unpack_complex(x_ref[:, j].reshape(b, d))
        v = jnp.dot(mf_ref[...], jnp.concatenate([xr, xi], axis=0), preferred_element_type=F32)
        vr, vi = v[:b], v[b:]
        kr, ki = kr_ref[j], ki_ref[j]
        yr = (vr * kr - vi * ki).astype(BF16)
        yi = (vr * ki + vi * kr).astype(BF16)
        z = jnp.dot(mi_ref[...], jnp.concatenate([yr, yi], axis=0), preferred_element_type=F32)
        zr, zi = z[:b], z[b:]
        cr, ci = cr_ref[j], ci_ref[j]
        z_ref[:, j] = _pack_complex(zr * cr - zi * ci, zr * ci + zi * cr).reshape(DFT_G, 8, d)


def _dft3_kernel(z_ref, fi_ref, y_ref):
    zr, zi = _unpack_complex(z_ref[...])
    y_ref[...] = jnp.dot(fi_ref[...], jnp.concatenate([zr, zi], axis=0), preferred_element_type=F32)


def _dft_constants(l):
    n = 2 * l
    b = DFT_B
    a = n // b
    ah = a // 2
    two_pi = 2.0 * math.pi
    f1 = jnp.arange(a, dtype=F32)[:, None]
    aa = jnp.arange(ah, dtype=F32)[None, :]
    ang_a = two_pi * ((f1 * aa) % a) / a
    eye8 = jnp.eye(8, dtype=F32)
    fa = jnp.kron(jnp.concatenate([jnp.cos(ang_a), -jnp.sin(ang_a)], axis=0), eye8).astype(BF16)
    fi = jnp.kron(jnp.concatenate([jnp.cos(ang_a).T, -jnp.sin(ang_a).T], axis=1), eye8).astype(BF16)
    bb = jnp.arange(b, dtype=F32)
    ang_b = two_pi * ((bb[:, None] * bb[None, :]) % b) / b
    cb, sb = jnp.cos(ang_b), jnp.sin(ang_b)
    mf = jnp.concatenate([jnp.concatenate([cb, sb], 1), jnp.concatenate([-sb, cb], 1)], 0).astype(BF16)
    mi = jnp.concatenate([jnp.concatenate([cb, -sb], 1), jnp.concatenate([sb, cb], 1)], 0).astype(BF16)
    ang_t = two_pi * (bb[:, None] * jnp.arange(a, dtype=F32)[None, :]) / n
    by_group = lambda m: m.reshape(DFT_G, 8, a).transpose(0, 2, 1).reshape(DFT_G, 8 * a, 1)
    twr, twi = by_group(jnp.cos(ang_t)), by_group(-jnp.sin(ang_t))
    cr, ci = jnp.cos(ang_t).T[..., None], jnp.sin(ang_t).T[..., None]
    return dict(a=a, ah=ah, fa=fa, fi=fi, mf=mf, mi=mi, twr=twr, twi=twi, cr=cr, ci=ci)


def _dft_stage1(vgm, c):
    bt, _, _, d = vgm.shape
    a, ah = c["a"], c["ah"]
    return pl.pallas_call(
        functools.partial(_dft1_kernel, a=a),
        grid=(bt, DFT_G),
        in_specs=[
            pl.BlockSpec((None, None, 8 * ah, d), lambda n_, g: (n_, g, 0, 0)),
            pl.BlockSpec((16 * a, 8 * ah), lambda n_, g: (0, 0)),
            pl.BlockSpec((None, 8 * a, 1), lambda n_, g: (g, 0, 0)),
            pl.BlockSpec((None, 8 * a, 1), lambda n_, g: (g, 0, 0)),
        ],
        out_specs=pl.BlockSpec((None, None, 8 * a, d), lambda n_, g: (n_, g, 0, 0)),
        out_shape=jax.ShapeDtypeStruct((bt, DFT_G, 8 * a, d), I32),
        compiler_params=_cparams(("parallel", "parallel")),
        name="dft_stage1",
    )(vgm, c["fa"], c["twr"], c["twi"])


SPEC_F1 = 4


def _filter_spec_kernel(x_ref, mf_ref, s_ref, kr_ref, ki_ref, *, b):
    d = x_ref.shape[-1]
    for j in range(SPEC_F1):
        spec = []
        for side in range(2):
            xr, xi = _unpack_complex(x_ref[side, :, j].reshape(b, d))
            spec.append(jnp.dot(mf_ref[...], jnp.concatenate([xr, xi], axis=0), preferred_element_type=F32))
        fwd, bwd = spec
        kr_ref[j] = (fwd[:b] + bwd[:b]) * s_ref[...]
        ki_ref[j] = (fwd[b:] - bwd[b:]) * s_ref[...]


def _filter_spectrum(h_fwd, h_bwd0, scale, c):
    l, d = h_fwd.shape
    a, ah, b = c["a"], c["ah"], DFT_B
    hgm = jnp.stack([h_fwd, h_bwd0]).reshape(2, ah, DFT_G, 8, d).transpose(0, 2, 1, 3, 4).reshape(2, DFT_G, 8 * ah, d)
    x = _dft_stage1(hgm, c)
    out = jax.ShapeDtypeStruct((a, b, d), F32)
    return pl.pallas_call(
        functools.partial(_filter_spec_kernel, b=b),
        grid=(a // SPEC_F1,),
        in_specs=[
            pl.BlockSpec((2, DFT_G, SPEC_F1, 8, d), lambda i: (0, 0, i, 0, 0)),
            pl.BlockSpec((2 * b, 2 * b), lambda i: (0, 0)),
            pl.BlockSpec((1, d), lambda i: (0, 0)),
        ],
        out_specs=[pl.BlockSpec((SPEC_F1, b, d), lambda i: (i, 0, 0))] * 2,
        out_shape=[out, out],
        compiler_params=_cparams(("parallel",)),
        name="filter_spectrum",
    )(x.reshape(2, DFT_G, a, 8, d), c["mf"], scale.reshape(1, d))


def _long_conv(vgm, kr, ki, c):
    bt, _, _, d = vgm.shape
    a, ah, b = c["a"], c["ah"], DFT_B
    mf, mi, cr, ci, fi = c["mf"], c["mi"], c["cr"], c["ci"], c["fi"]
    x = _dft_stage1(vgm, c)

    at = SPEC_F1
    z = pl.pallas_call(
        functools.partial(_dft2_kernel, at=at, b=b),
        grid=(a // at, bt),
        in_specs=[
            pl.BlockSpec((None, DFT_G, at, 8, d), lambda i, n_: (n_, 0, i, 0, 0)),
            pl.BlockSpec((2 * b, 2 * b), lambda i, n_: (0, 0)),
            pl.BlockSpec((2 * b, 2 * b), lambda i, n_: (0, 0)),
            pl.BlockSpec((at, b, d), lambda i, n_: (i, 0, 0)),
            pl.BlockSpec((at, b, d), lambda i, n_: (i, 0, 0)),
            pl.BlockSpec((at, b, 1), lambda i, n_: (i, 0, 0)),
            pl.BlockSpec((at, b, 1), lambda i, n_: (i, 0, 0)),
        ],
        out_specs=pl.BlockSpec((None, DFT_G, at, 8, d), lambda i, n_: (n_, 0, i, 0, 0)),
        out_shape=jax.ShapeDtypeStruct((bt, DFT_G, a, 8, d), I32),
        compiler_params=_cparams(("parallel", "parallel")),
        name="dft_stage2",
    )(x.reshape(bt, DFT_G, a, 8, d), mf, mi, kr, ki, cr, ci)

    return pl.pallas_call(
        _dft3_kernel,
        grid=(bt, DFT_G),
        in_specs=[
            pl.BlockSpec((None, None, 8 * a, d), lambda n_, g: (n_, g, 0, 0)),
            pl.BlockSpec((8 * ah, 16 * a), lambda n_, g: (0, 0)),
        ],
        out_specs=pl.BlockSpec((None, None, 8 * ah, d), lambda n_, g: (n_, g, 0, 0)),
        out_shape=jax.ShapeDtypeStruct((bt, DFT_G, 8 * ah, d), F32),
        compiler_params=_cparams(("parallel", "parallel")),
        name="dft_stage3",
    )(z.reshape(bt, DFT_G, 8 * a, d), fi)


def _hyena_filter_taps(l, w_f0, b_f0, w_fin, b_fin, w_fout, sin_freq):
    hp = lax.Precision.HIGHEST
    d = w_fout.shape[-1] // 2
    t = jnp.linspace(0.0, 1.0, l, dtype=F32)[:, None]
    bands = (EMB_DIM - 1) // 2
    f = jnp.linspace(1e-4, bands - 1, bands, dtype=F32)[None, :]
    w = 2.0 * math.pi * jnp.arange(l, dtype=F32)[:, None] / l
    z = jnp.concatenate([t, jnp.cos(f * w), -jnp.sin(f * w)], axis=-1)
    hf = jnp.sin(sin_freq * (jnp.dot(z, w_f0, precision=hp) + b_f0))
    for i in range(N_INNER_MLPS):
        hf = jnp.sin(sin_freq * (jnp.dot(hf, w_fin[i], precision=hp) + b_fin[i]))
    hf = jnp.dot(hf, w_fout, precision=hp)
    max_decay = math.log(DECAY_TARGET) / FAST_DECAY_PCT
    min_decay = math.log(DECAY_TARGET) / SLOW_DECAY_PCT
    deltas = jnp.linspace(min_decay, max_decay, d, dtype=F32)[None, :]
    decay = jnp.exp(-t * jnp.abs(deltas))
    h_fwd = hf[:, :d] * decay
    lag = lax.broadcasted_iota(I32, (l, 1), 0)
    h_bwd0 = jnp.where(lag > 0, hf[:, d:] * decay, 0.0)
    norm = jnp.sum(jnp.abs(h_fwd), axis=0) + jnp.sum(jnp.abs(h_bwd0), axis=0)
    return h_fwd, h_bwd0, 1.0 / (norm * (2 * l))


def _hyena_tail_kernel(y_ref, v_ref, x0_ref, fb_ref, w_ref, b_ref, x_ref, o_ref, *, tm):
    y = _from_group_major(y_ref, tm)
    g = ((y + v_ref[...] * fb_ref[...]) * x0_ref[...]).astype(BF16)
    o_ref[...] = jnp.dot(g, w_ref[...], preferred_element_type=F32) + b_ref[...] + x_ref[...]


def _hyena_tail(ygm, v, x0, fft_bias, w_out, b_out, x):
    bt, l, d = x.shape
    tm = ROW_TILE
    tile = pl.BlockSpec((None, tm, d), lambda b, i: (b, i, 0))
    rowv = pl.BlockSpec((1, d), lambda b, i: (0, 0))
    return pl.pallas_call(
        functools.partial(_hyena_tail_kernel, tm=tm),
        grid=(bt, l // tm),
        in_specs=[pl.BlockSpec((None, DFT_G, 8 * tm // DFT_B, d), lambda b, i: (b, 0, i, 0)),
                  tile, tile, rowv, pl.BlockSpec((d, d), lambda b, i: (0, 0)), rowv, tile],
        out_specs=tile,
        out_shape=jax.ShapeDtypeStruct((bt, l, d), F32),
        compiler_params=_cparams(("parallel", "parallel")),
        name="hyena_tail",
    )(ygm, v, x0, fft_bias.reshape(1, d), w_out, b_out.reshape(1, d), x)


def _router_kernel(x_ref, g_ref, wt_ref, hn_ref, pt_ref):
    h = _rms(x_ref[...], g_ref[...])
    hn_ref[...] = h.astype(BF16)
    lt = lax.dot_general(wt_ref[...], h, (((1,), (1,)), ((), ())), precision=lax.Precision.HIGHEST,
                         preferred_element_type=F32)
    et = jnp.exp(lt - jnp.max(lt, axis=0, keepdims=True))
    pt_ref[...] = et / jnp.sum(et, axis=0, keepdims=True)


def _router(x, g, w_router):
    t, d = x.shape
    e = w_router.shape[1]
    return pl.pallas_call(
        _router_kernel,
        grid=(t // ROW_TILE,),
        in_specs=[
            pl.BlockSpec((ROW_TILE, d), lambda i: (i, 0)),
            pl.BlockSpec((1, d), lambda i: (0, 0)),
            pl.BlockSpec((e, d), lambda i: (0, 0)),
        ],
        out_specs=[
            pl.BlockSpec((ROW_TILE, d), lambda i: (i, 0)),
            pl.BlockSpec((e, ROW_TILE), lambda i: (0, i)),
        ],
        out_shape=[
            jax.ShapeDtypeStruct((t, d), BF16),
            jax.ShapeDtypeStruct((e, t), F32),
        ],
        compiler_params=_cparams(("parallel",)),
        name="router",
    )(x, g.reshape(1, d), w_router.T)


CUM_BLK = 256


def _select_kernel(p_ref, posm_ref, posx_ref, mask_ref, *, cap, t):
    keys = pltpu.bitcast(p_ref[...], I32)
    e = keys.shape[0]

    def vbit(i, tau):
        cand = tau | jnp.left_shift(jnp.int32(1), 30 - i)
        cnt = jnp.sum((keys >= cand).astype(I32), axis=1, keepdims=True)
        return jnp.where(cnt >= cap, cand, tau)

    tau = lax.fori_loop(0, 31, vbit, jnp.zeros((e, 1), I32))
    gt = keys > tau
    eq = keys == tau
    quota = cap - jnp.sum(gt.astype(I32), axis=1, keepdims=True)
    idx = lax.broadcasted_iota(I32, keys.shape, 1)
    nbits = max(1, (t - 1).bit_length())

    def ibit(i, jj):
        cand = jj | jnp.left_shift(jnp.int32(1), nbits - 1 - i)
        cnt = jnp.sum((eq & (idx < cand)).astype(I32), axis=1, keepdims=True)
        return jnp.where(cnt < quota, cand, jj)

    jlast = lax.fori_loop(0, nbits, ibit, jnp.zeros((e, 1), I32))
    mask = gt | (eq & (idx <= jlast))
    mask_ref[...] = mask.astype(F32)

    r = lax.broadcasted_iota(I32, (CUM_BLK, CUM_BLK), 0)
    c = lax.broadcasted_iota(I32, (CUM_BLK, CUM_BLK), 1)
    tri = (r <= c).astype(BF16)
    run = jnp.zeros((e, 1), F32)
    for blk in range(t // CUM_BLK):
        cols = pl.ds(blk * CUM_BLK, CUM_BLK)
        m = mask_ref[:, cols]
        incl = jnp.dot(m.astype(BF16), tri, preferred_element_type=F32)
        pos = (incl - m + run).astype(I32)
        posx_ref[:, cols] = pos
        posm_ref[:, cols] = jnp.where(m > 0.0, pos, -1)
        run = run + incl[:, CUM_BLK - 1:CUM_BLK]


def _select(probs_t, cap):
    e, t = probs_t.shape
    full = pl.BlockSpec((e, t), lambda: (0, 0))
    return pl.pallas_call(
        functools.partial(_select_kernel, cap=cap, t=t),
        in_specs=[full],
        out_specs=[full, full],
        out_shape=[jax.ShapeDtypeStruct((e, t), I32), jax.ShapeDtypeStruct((e, t), I32)],
        scratch_shapes=[pltpu.VMEM((e, t), F32)],
        compiler_params=pltpu.CompilerParams(vmem_limit_bytes=VMEM_LIMIT),
        name="select",
    )(probs_t)


GATHER_EXPERTS = 2


def _gather_kernel(st_ref, hn_ref, posm_ref, o_ref, *, sub):
    eg = pl.program_id(0)
    k = pl.program_id(1)

    @pl.when(k == 0)
    def _():
        o_ref[...] = jnp.zeros_like(o_ref)

    rows = lax.broadcasted_iota(I32, (GATHER_WIN, GATHER_BLK), 0)
    for j in range(sub):
        hn = hn_ref[pl.ds(j * GATHER_BLK, GATHER_BLK), :]
        for x in range(GATHER_EXPERTS):
            base = pl.multiple_of(st_ref[eg * GATHER_EXPERTS + x, k * sub + j], BF16_SUBLANE_TILE)
            rel = posm_ref[x, :, pl.ds(j * GATHER_BLK, GATHER_BLK)] - base
            onehot = (rows == rel).astype(BF16)
            slab = jnp.dot(onehot, hn, preferred_element_type=F32).astype(BF16)
            win = pl.ds(base, GATHER_WIN)
            o_ref[x, win, :] = o_ref[x, win, :] + slab


def _gather_tokens(hn, posm, starts, cap):
    t, d = hn.shape
    e = posm.shape[0]
    sub = 4
    nb = t // (GATHER_BLK * sub)
    cp = cap + GATHER_WIN
    return pl.pallas_call(
        functools.partial(_gather_kernel, sub=sub),
        grid_spec=pltpu.PrefetchScalarGridSpec(
            num_scalar_prefetch=1,
            grid=(e // GATHER_EXPERTS, nb),
            in_specs=[
                pl.BlockSpec((GATHER_BLK * sub, d), lambda ei, k, st: (k, 0)),
                pl.BlockSpec((GATHER_EXPERTS, 1, GATHER_BLK * sub), lambda ei, k, st: (ei, 0, k)),
            ],
            out_specs=pl.BlockSpec((GATHER_EXPERTS, cp, d), lambda ei, k, st: (ei, 0, 0)),
        ),
        out_shape=jax.ShapeDtypeStruct((e, cp, d), BF16),
        compiler_params=_cparams(("arbitrary", "arbitrary")),
        name="gather_tokens",
    )(starts, hn, posm.reshape(e, 1, t))


def _ffn_kernel(x_ref, wg_ref, wu_ref, wd_ref, o_ref, *, nrt, splits):
    r = pl.program_id(1)

    @pl.when(r < nrt)
    def _():
        x = x_ref[...]
        dn = None
        for c0, c1 in splits:
            g = jnp.dot(x, wg_ref[:, c0:c1], preferred_element_type=F32)
            u = jnp.dot(x, wu_ref[:, c0:c1], preferred_element_type=F32)
            h = (g * jax.nn.sigmoid(g) * u).astype(BF16)
            part = jnp.dot(h, wd_ref[c0:c1, :], preferred_element_type=F32)
            dn = part if dn is None else dn + part
        o_ref[...] = dn.astype(BF16)

    @pl.when(r == nrt)
    def _():
        o_ref[...] = jnp.zeros_like(o_ref)


def _expert_ffn(xs, w_gate, w_up, w_down, cap):
    e, _, d = xs.shape
    f = w_gate.shape[2]
    nrt = cap // FFN_ROWS
    half = (f // MXU_DIM + 1) // 2 * MXU_DIM
    splits = ((0, half), (half, f))
    row = lambda r: jnp.minimum(r, nrt - 1)
    return pl.pallas_call(
        functools.partial(_ffn_kernel, nrt=nrt, splits=splits),
        grid=(e, nrt + 1),
        in_specs=[
            pl.BlockSpec((None, FFN_ROWS, d), lambda ei, r: (ei, row(r), 0)),
            pl.BlockSpec((None, d, f), lambda ei, r: (ei, 0, 0)),
            pl.BlockSpec((None, d, f), lambda ei, r: (ei, 0, 0)),
            pl.BlockSpec((None, f, d), lambda ei, r: (ei, 0, 0)),
        ],
        out_specs=pl.BlockSpec((None, FFN_ROWS, d), lambda ei, r: (ei, r, 0)),
        out_shape=jax.ShapeDtypeStruct((e, cap + FFN_ROWS, d), BF16),
        compiler_params=_cparams(("parallel", "arbitrary")),
        name="expert_ffn",
    )(xs, w_gate, w_up, w_down)


def _combine_kernel(st_ref, nch_ref, x_ref, posm_ref, p_ref, g_ref, out_hbm, o_ref,
                    stack, lhs, xbuf, sem, xsem, *, ne, cp, final_norm):
    b = pl.program_id(0)
    nsteps = pl.num_programs(0)

    def window_copies(step, slot):
        for ei in range(ne):
            start = pl.multiple_of(ei * cp + st_ref[ei, step], BF16_SUBLANE_TILE)
            for part in range(2):
                dst = stack.at[slot, pl.ds((2 * ei + part) * COMB_WIN, COMB_WIN), :]
                yield pltpu.make_async_copy(out_hbm.at[pl.ds(start, COMB_WIN), :], dst, sem.at[slot])

    @pl.when(b == 0)
    def _():
        for c in window_copies(0, 0):
            c.start()

    @pl.when(b + 1 < nsteps)
    def _():
        for c in window_copies(b + 1, (b + 1) % 2):
            c.start()

    cols = lax.broadcasted_iota(I32, (COMB_BLK, COMB_WIN), 1)

    def weighted_onehot(ei, chunk):
        rel = posm_ref[:, ei:ei + 1] - st_ref[ei, b] - chunk * COMB_WIN
        return jnp.where(cols == rel, p_ref[:, ei:ei + 1], 0.0)

    for ei in range(ne):
        v = weighted_onehot(ei, 0)
        hi = v.astype(BF16)
        lhs[:, pl.ds(2 * ei * COMB_WIN, COMB_WIN)] = hi
        lhs[:, pl.ds((2 * ei + 1) * COMB_WIN, COMB_WIN)] = (v - hi.astype(F32)).astype(BF16)

    slot = b % 2
    for c in window_copies(b, slot):
        c.wait()
    o_ref[...] = x_ref[...] + jnp.dot(lhs[...], stack[slot], preferred_element_type=F32)

    for ei in range(ne):
        def extra(chunk, carry):
            start = pl.multiple_of(ei * cp + st_ref[ei, b] + chunk * COMB_WIN, BF16_SUBLANE_TILE)
            cpy = pltpu.make_async_copy(out_hbm.at[pl.ds(start, COMB_WIN), :], xbuf, xsem)
            cpy.start()
            cpy.wait()
            v = weighted_onehot(ei, chunk)
            hi = v.astype(BF16)
            lo = (v - hi.astype(F32)).astype(BF16)
            rows = xbuf[...]
            o_ref[...] = (o_ref[...] + jnp.dot(hi, rows, preferred_element_type=F32)
                          + jnp.dot(lo, rows, preferred_element_type=F32))
            return carry

        lax.fori_loop(1, nch_ref[ei, b], extra, 0)

    if final_norm:
        o_ref[...] = _rms(o_ref[...], g_ref[...])


def _combine(x, posm_tok, probs, out_rows, starts, nchunks, final_g):
    t, d = x.shape
    e = posm_tok.shape[1]
    cp = out_rows.shape[0] // e
    final_norm = final_g is not None
    g = final_g if final_norm else jnp.ones((d,), F32)
    return pl.pallas_call(
        functools.partial(_combine_kernel, ne=e, cp=cp, final_norm=final_norm),
        grid_spec=pltpu.PrefetchScalarGridSpec(
            num_scalar_prefetch=2,
            grid=(t // COMB_BLK,),
            in_specs=[
                pl.BlockSpec((COMB_BLK, d), lambda i, st, nc: (i, 0)),
                pl.BlockSpec((COMB_BLK, e), lambda i, st, nc: (i, 0)),
                pl.BlockSpec((COMB_BLK, e), lambda i, st, nc: (i, 0)),
                pl.BlockSpec((1, d), lambda i, st, nc: (0, 0)),
                pl.BlockSpec(memory_space=pl.ANY),
            ],
            out_specs=pl.BlockSpec((COMB_BLK, d), lambda i, st, nc: (i, 0)),
            scratch_shapes=[
                pltpu.VMEM((2, 2 * e * COMB_WIN, d), BF16),
                pltpu.VMEM((COMB_BLK, 2 * e * COMB_WIN), BF16),
                pltpu.VMEM((COMB_WIN, d), BF16),
                pltpu.SemaphoreType.DMA((2,)),
                pltpu.SemaphoreType.DMA,
            ],
        ),
        out_shape=jax.ShapeDtypeStruct((t, d), F32),
        compiler_params=_cparams(("arbitrary",)),
        name="combine",
    )(starts, nchunks, x, posm_tok, probs, g.reshape(1, d), out_rows)


def _ec_moe(x, g, w_router, w_gate, w_up, w_down, final_g):
    t, d = x.shape
    e = w_router.shape[1]
    cap = max(1, min(t, CAPACITY_FACTOR * t // e))
    hn, probs_t = _router(x, g, w_router)
    posm, posx = _select(probs_t, cap)
    tile16 = lambda c: (c // BF16_SUBLANE_TILE) * BF16_SUBLANE_TILE
    gather_starts = tile16(posx[:, ::GATHER_BLK])
    cum = posx[:, ::COMB_BLK]
    cnt = jnp.concatenate([cum[:, 1:], jnp.full((e, 1), cap, I32)], axis=1) - cum
    comb_starts = tile16(cum)
    nchunks = (cum - comb_starts + cnt + COMB_WIN - 1) // COMB_WIN
    xs = _gather_tokens(hn, posm, gather_starts, cap)
    out_rows = _expert_ffn(xs, w_gate, w_up, w_down, cap)
    return _combine(x, posm.T, probs_t.T, out_rows.reshape(-1, d), comb_starts, nchunks, final_g)


def _encoder(x3, p):
    bt, l, d = x3.shape
    t = bt * l
    x = x3.reshape(t, d)
    depth = p["norm_mix_g"].shape[0]
    for i in range(depth):
        j = i // 2
        if i % 2 == 0:
            u = _norm_glu(x, p["norm_mix_g"][i], p["conf_w_pw1"][j], p["conf_b_pw1"][j])
            x = _conf_tail(u.reshape(bt, l, d), x.reshape(bt, l, d), p["conf_w_dw"][j], p["conf_b_dw"][j],
                           p["conf_ln_g"][j], p["conf_ln_b"][j], p["conf_w_pw2"][j],
                           p["conf_b_pw2"][j]).reshape(t, d)
        else:
            u3 = _norm_proj(x, p["norm_mix_g"][i], p["hy_w_in"][j], p["hy_b_in"][j], tn=3 * d)
            vg, vgm, x0 = _short_conv(u3.reshape(bt, l, 3 * d), p["hy_w_short"][j], p["hy_b_short"][j])
            c = _dft_constants(l)
            taps = _hyena_filter_taps(l, p["hy_w_f0"][j], p["hy_b_f0"][j], p["hy_w_fin"][j],
                                      p["hy_b_fin"][j], p["hy_w_fout"][j], p["hy_sin_freq"][j])
            kr, ki = _filter_spectrum(*taps, c)
            ygm = _long_conv(vgm, kr, ki, c)
            x = _hyena_tail(ygm, vg, x0, p["hy_fft_bias"][j], p["hy_w_out"][j], p["hy_b_out"][j],
                            x.reshape(bt, l, d)).reshape(t, d)
        final_g = p["final_norm_g"] if i == depth - 1 else None
        x = _ec_moe(x, p["norm_ffn_g"][i], p["moe_w_router"][i], p["moe_w_gate"][i], p["moe_w_up"][i],
                    p["moe_w_down"][i], final_g)
    return x.reshape(bt, l, d)


_BF16_WEIGHTS = ("conf_w_pw1", "conf_w_pw2", "hy_w_in", "hy_w_out", "moe_w_gate", "moe_w_up", "moe_w_down")


def kernel(x_prompt, x_sample, norm_mix_g, norm_ffn_g, final_norm_g, conf_w_pw1, conf_b_pw1, conf_w_dw, conf_b_dw, conf_ln_g, conf_ln_b, conf_w_pw2, conf_b_pw2, hy_w_in, hy_b_in, hy_w_short, hy_b_short, hy_w_f0, hy_b_f0, hy_w_fin, hy_b_fin, hy_w_fout, hy_sin_freq, hy_fft_bias, hy_w_out, hy_b_out, moe_w_router, moe_w_gate, moe_w_up, moe_w_down):
    p = dict(norm_mix_g=norm_mix_g, norm_ffn_g=norm_ffn_g, final_norm_g=final_norm_g,
             conf_w_pw1=conf_w_pw1, conf_b_pw1=conf_b_pw1, conf_w_dw=conf_w_dw, conf_b_dw=conf_b_dw,
             conf_ln_g=conf_ln_g, conf_ln_b=conf_ln_b, conf_w_pw2=conf_w_pw2, conf_b_pw2=conf_b_pw2,
             hy_w_in=hy_w_in, hy_b_in=hy_b_in, hy_w_short=hy_w_short, hy_b_short=hy_b_short,
             hy_w_f0=hy_w_f0, hy_b_f0=hy_b_f0, hy_w_fin=hy_w_fin, hy_b_fin=hy_b_fin, hy_w_fout=hy_w_fout,
             hy_sin_freq=hy_sin_freq, hy_fft_bias=hy_fft_bias, hy_w_out=hy_w_out, hy_b_out=hy_b_out,
             moe_w_router=moe_w_router, moe_w_gate=moe_w_gate, moe_w_up=moe_w_up, moe_w_down=moe_w_down)
    for name in _BF16_WEIGHTS:
        p[name] = p[name].astype(BF16)
    return (_encoder(x_prompt, p), _encoder(x_sample, p))
```

```python
import functools
import math

import jax
import jax.numpy as jnp
from jax import lax
from jax.experimental import pallas as pl
from jax.experimental.pallas import tpu as pltpu

F32 = jnp.float32
BF16 = jnp.bfloat16
I32 = jnp.int32

RMS_EPS = 1e-6
LN_EPS = 1e-5
CONV_WIDTH = 31
SHORT_WIDTH = 3
EMB_DIM = 33
N_INNER_MLPS = 2
FAST_DECAY_PCT = 0.3
SLOW_DECAY_PCT = 1.5
DECAY_TARGET = 1e-2
N_EXPERTS = 16
CAPACITY_FACTOR = 2

LANE = 128
BF16_SUBLANE_TILE = 16
MXU_DIM = 256
VMEM_LIMIT = 56 * 1024 * 1024

ROW_TILE = 512
DFT_B = 128
GATHER_BLK = 256
GATHER_WIN = GATHER_BLK + BF16_SUBLANE_TILE
COMB_BLK = 512
COMB_WIN = 128
FFN_ROWS = 512


def _cparams(sem):
    return pltpu.CompilerParams(dimension_semantics=sem, vmem_limit_bytes=VMEM_LIMIT)


def _rms(x, g):
    return x * lax.rsqrt(jnp.mean(x * x, axis=-1, keepdims=True) + RMS_EPS) * g


def _norm_proj_kernel(x_ref, g_ref, w_ref, b_ref, o_ref):
    h = _rms(x_ref[...], g_ref[...]).astype(BF16)
    o_ref[...] = jnp.dot(h, w_ref[...], preferred_element_type=F32) + b_ref[...]


def _norm_glu_kernel(x_ref, g_ref, wa_ref, wg_ref, ba_ref, bg_ref, o_ref):
    h = _rms(x_ref[...], g_ref[...]).astype(BF16)
    a = jnp.dot(h, wa_ref[...], preferred_element_type=F32) + ba_ref[...]
    g = jnp.dot(h, wg_ref[...], preferred_element_type=F32) + bg_ref[...]
    o_ref[...] = a * jax.nn.sigmoid(g)


def _norm_proj(x, g, w, b, tn):
    t, d = x.shape
    n = w.shape[1]
    return pl.pallas_call(
        _norm_proj_kernel,
        grid=(t // ROW_TILE, n // tn),
        in_specs=[
            pl.BlockSpec((ROW_TILE, d), lambda i, j: (i, 0)),
            pl.BlockSpec((1, d), lambda i, j: (0, 0)),
            pl.BlockSpec((d, tn), lambda i, j: (0, j)),
            pl.BlockSpec((1, tn), lambda i, j: (0, j)),
        ],
        out_specs=pl.BlockSpec((ROW_TILE, tn), lambda i, j: (i, j)),
        out_shape=jax.ShapeDtypeStruct((t, n), F32),
        compiler_params=_cparams(("parallel", "parallel")),
        name="norm_proj",
    )(x, g.reshape(1, d), w, b.reshape(1, n))


def _norm_glu(x, g, w, b):
    t, d = x.shape
    n = w.shape[1] // 2
    return pl.pallas_call(
        _norm_glu_kernel,
        grid=(t // ROW_TILE,),
        in_specs=[
            pl.BlockSpec((ROW_TILE, d), lambda i: (i, 0)),
            pl.BlockSpec((1, d), lambda i: (0, 0)),
            pl.BlockSpec((d, n), lambda i: (0, 0)),
            pl.BlockSpec((d, n), lambda i: (0, 1)),
            pl.BlockSpec((1, n), lambda i: (0, 0)),
            pl.BlockSpec((1, n), lambda i: (0, 1)),
        ],
        out_specs=pl.BlockSpec((ROW_TILE, n), lambda i: (i, 0)),
        out_shape=jax.ShapeDtypeStruct((t, n), F32),
        compiler_params=_cparams(("parallel",)),
        name="norm_glu",
    )(x, g.reshape(1, d), w, w, b.reshape(1, 2 * n), b.reshape(1, 2 * n))


HALO = 16


CONV_CHUNK = 16


def _conf_tail_kernel(u_ref, up_ref, un_ref, x_ref, wdw_ref, bdw_ref, lng_ref, lnb_ref,
                      w2_ref, b2_ref, o_ref, ext_ref, acc_ref, *, tm, nblk):
    i = pl.program_id(1)
    d = u_ref.shape[-1]
    sl = d // LANE
    prev = up_ref[...].reshape(HALO, sl, LANE)
    nxt = un_ref[...].reshape(HALO, sl, LANE)
    ext_ref[pl.ds(0, HALO)] = jnp.where(i > 0, prev, jnp.zeros_like(prev))
    ext_ref[pl.ds(HALO, tm)] = u_ref[...].reshape(tm, sl, LANE)
    ext_ref[pl.ds(HALO + tm, HALO)] = jnp.where(i < nblk - 1, nxt, jnp.zeros_like(nxt))
    off = HALO - CONV_WIDTH // 2

    def chunk(c, carry):
        r0 = pl.multiple_of(c * CONV_CHUNK, CONV_CHUNK)
        acc = jnp.broadcast_to(bdw_ref[...], (CONV_CHUNK, sl, LANE))
        for k in range(CONV_WIDTH):
            acc = acc + ext_ref[pl.ds(r0 + off + k, CONV_CHUNK)] * wdw_ref[pl.ds(k, 1)]
        acc_ref[pl.ds(r0, CONV_CHUNK)] = acc
        return carry

    lax.fori_loop(0, tm // CONV_CHUNK, chunk, 0)
    acc = acc_ref[...].reshape(tm, d)
    mu = jnp.mean(acc, axis=-1, keepdims=True)
    xc = acc - mu
    var = jnp.mean(xc * xc, axis=-1, keepdims=True)
    y = xc * lax.rsqrt(var + LN_EPS) * lng_ref[...] + lnb_ref[...]
    y = (y * jax.nn.sigmoid(y)).astype(BF16)
    o_ref[...] = jnp.dot(y, w2_ref[...], preferred_element_type=F32) + b2_ref[...] + x_ref[...]


def _conf_tail(u, x, w_dw, b_dw, ln_g, ln_b, w2, b2):
    bt, l, d = u.shape
    tm = ROW_TILE
    nblk = l // tm
    hb = tm // HALO
    sl = d // LANE
    wpad = jnp.zeros((32, d), F32).at[:CONV_WIDTH].set(w_dw).reshape(32, sl, LANE)
    row = lambda a: a.reshape(1, d)
    return pl.pallas_call(
        functools.partial(_conf_tail_kernel, tm=tm, nblk=nblk),
        grid=(bt, nblk),
        in_specs=[
            pl.BlockSpec((None, tm, d), lambda b, i: (b, i, 0)),
            pl.BlockSpec((None, HALO, d), lambda b, i: (b, jnp.maximum(i * hb - 1, 0), 0)),
            pl.BlockSpec((None, HALO, d), lambda b, i: (b, jnp.minimum((i + 1) * hb, nblk * hb - 1), 0)),
            pl.BlockSpec((None, tm, d), lambda b, i: (b, i, 0)),
            pl.BlockSpec((32, sl, LANE), lambda b, i: (0, 0, 0)),
            pl.BlockSpec((1, sl, LANE), lambda b, i: (0, 0, 0)),
            pl.BlockSpec((1, d), lambda b, i: (0, 0)),
            pl.BlockSpec((1, d), lambda b, i: (0, 0)),
            pl.BlockSpec((d, d), lambda b, i: (0, 0)),
            pl.BlockSpec((1, d), lambda b, i: (0, 0)),
        ],
        out_specs=pl.BlockSpec((None, tm, d), lambda b, i: (b, i, 0)),
        out_shape=jax.ShapeDtypeStruct((bt, l, d), F32),
        scratch_shapes=[pltpu.VMEM((tm + 2 * HALO, sl, LANE), F32), pltpu.VMEM((tm, sl, LANE), F32)],
        compiler_params=_cparams(("parallel", "parallel")),
        name="conf_tail",
    )(u, u, u, x, wpad, b_dw.reshape(1, sl, LANE), row(ln_g), row(ln_b), w2, row(b2))


SHALO = 8


def _short_conv_kernel(u_ref, up_ref, un_ref, w_ref, b_ref, vg_ref, vgm_ref, x0_ref, ext_ref, *, tm, nblk, d):
    i = pl.program_id(1)
    prev = up_ref[...]
    nxt = un_ref[...]
    ext_ref[pl.ds(0, SHALO), :] = jnp.where(i > 0, prev, jnp.zeros_like(prev))
    ext_ref[pl.ds(SHALO, tm), :] = u_ref[...]
    ext_ref[pl.ds(SHALO + tm, SHALO), :] = jnp.where(i < nblk - 1, nxt, jnp.zeros_like(nxt))

    def conv(c):
        cols = pl.ds(c * d, d)
        acc = b_ref[:, cols]
        for k in range(SHORT_WIDTH):
            acc = acc + ext_ref[pl.ds(SHALO - 1 + k, tm), cols] * w_ref[pl.ds(k, 1), cols]
        return acc

    x0_ref[...] = conv(0)
    vg = conv(2) * conv(1)
    vg_ref[...] = vg
    _to_group_major(vg, vgm_ref, tm)


DFT_G = DFT_B // 8


def _to_group_major(val, ref, tm):
    for ai in range(tm // DFT_B):
        for g in range(DFT_G):
            ref[g, pl.ds(ai * 8, 8), :] = val[ai * DFT_B + g * 8: ai * DFT_B + g * 8 + 8, :]


def _from_group_major(ref, tm):
    return jnp.concatenate([ref[g, pl.ds(ai * 8, 8), :] for ai in range(tm // DFT_B) for g in range(DFT_G)],
                           axis=0)


def _short_conv(u3, w_short, b_short):
    bt, l, d3 = u3.shape
    d = d3 // 3
    tm = ROW_TILE
    nblk = l // tm
    hb = tm // SHALO
    gm_rows = 8 * tm // DFT_B
    wpad = jnp.zeros((8, d3), F32).at[:SHORT_WIDTH].set(w_short)
    out = jax.ShapeDtypeStruct((bt, l, d), F32)
    return pl.pallas_call(
        functools.partial(_short_conv_kernel, tm=tm, nblk=nblk, d=d),
        grid=(bt, nblk),
        in_specs=[
            pl.BlockSpec((None, tm, d3), lambda b, i: (b, i, 0)),
            pl.BlockSpec((None, SHALO, d3), lambda b, i: (b, jnp.maximum(i * hb - 1, 0), 0)),
            pl.BlockSpec((None, SHALO, d3), lambda b, i: (b, jnp.minimum((i + 1) * hb, nblk * hb - 1), 0)),
            pl.BlockSpec((8, d3), lambda b, i: (0, 0)),
            pl.BlockSpec((1, d3), lambda b, i: (0, 0)),
        ],
        out_specs=[
            pl.BlockSpec((None, tm, d), lambda b, i: (b, i, 0)),
            pl.BlockSpec((None, DFT_G, gm_rows, d), lambda b, i: (b, 0, i, 0)),
            pl.BlockSpec((None, tm, d), lambda b, i: (b, i, 0)),
        ],
        out_shape=[out, jax.ShapeDtypeStruct((bt, DFT_G, 8 * l // DFT_B, d), F32), out],
        scratch_shapes=[pltpu.VMEM((tm + 2 * SHALO, d3), F32)],
        compiler_params=_cparams(("parallel", "parallel")),
        name="short_conv",
    )(u3, u3, u3, wpad, b_short.reshape(1, d3))


_HI16 = -65536


def _pack_complex(re, im):
    hi = lax.bitcast_convert_type(re.astype(BF16).astype(F32), I32)
    lo = lax.bitcast_convert_type(im.astype(BF16).astype(F32), I32)
    return hi | lax.shift_right_logical(lo, jnp.int32(16))


def _unpack_complex(w):
    re = lax.bitcast_convert_type(w & jnp.int32(_HI16), F32)
    im = lax.bitcast_convert_type(lax.shift_left(w, jnp.int32(16)), F32)
    return re.astype(BF16), im.astype(BF16)


def _dft1_kernel(v_ref, fa_ref, twr_ref, twi_ref, x_ref, *, a):
    r = jnp.dot(fa_ref[...], v_ref[...].astype(BF16), preferred_element_type=F32)
    re, im = r[:8 * a], r[8 * a:]
    twr, twi = twr_ref[...], twi_ref[...]
    x_ref[...] = _pack_complex(re * twr - im * twi, re * twi + im * twr)


def _dft2_kernel(x_ref, mf_ref, mi_ref, kr_ref, ki_ref, cr_ref, ci_ref, z_ref, *, at, b):
    d = x_ref.shape[-1]
    for j in range(at):
        xr, xi = _unpack_complex(x_ref[:, j].reshape(b, d))
        v = jnp.dot(mf_ref[...], jnp.concatenate([xr, xi], axis=0), preferred_element_type=F32)
        vr, vi = v[:b], v[b:]
        kr, ki = kr_ref[j], ki_ref[j]
        yr = (vr * kr - vi * ki).astype(BF16)
        yi = (vr * ki + vi * kr).astype(BF16)
        z = jnp.dot(mi_ref[...], jnp.concatenate([yr, yi], axis=0), preferred_element_type=F32)
        zr, zi = z[:b], z[b:]
        cr, ci = cr_ref[j], ci_ref[j]
        z_ref[:, j] = _pack_complex(zr * cr - zi * ci, zr * ci + zi * cr).reshape(DFT_G, 8, d)


def _dft3_kernel(z_ref, fi_ref, y_ref):
    zr, zi = _unpack_complex(z_ref[...])
    y_ref[...] = jnp.dot(fi_ref[...], jnp.concatenate([zr, zi], axis=0), preferred_element_type=F32)


def _dft_constants(l):
    n = 2 * l
    b = DFT_B
    a = n // b
    ah = a // 2
    two_pi = 2.0 * math.pi
    f1 = jnp.arange(a, dtype=F32)[:, None]
    aa = jnp.arange(ah, dtype=F32)[None, :]
    ang_a = two_pi * ((f1 * aa) % a) / a
    eye8 = jnp.eye(8, dtype=F32)
    fa = jnp.kron(jnp.concatenate([jnp.cos(ang_a), -jnp.sin(ang_a)], axis=0), eye8).astype(BF16)
    fi = jnp.kron(jnp.concatenate([jnp.cos(ang_a).T, -jnp.sin(ang_a).T], axis=1), eye8).astype(BF16)
    bb = jnp.arange(b, dtype=F32)
    ang_b = two_pi * ((bb[:, None] * bb[None, :]) % b) / b
    cb, sb = jnp.cos(ang_b), jnp.sin(ang_b)
    mf = jnp.concatenate([jnp.concatenate([cb, sb], 1), jnp.concatenate([-sb, cb], 1)], 0).astype(BF16)
    mi = jnp.concatenate([jnp.concatenate([cb, -sb], 1), jnp.concatenate([sb, cb], 1)], 0).astype(BF16)
    ang_t = two_pi * (bb[:, None] * jnp.arange(a, dtype=F32)[None, :]) / n
    by_group = lambda m: m.reshape(DFT_G, 8, a).transpose(0, 2, 1).reshape(DFT_G, 8 * a, 1)
    twr, twi = by_group(jnp.cos(ang_t)), by_group(-jnp.sin(ang_t))
    cr, ci = jnp.cos(ang_t).T[..., None], jnp.sin(ang_t).T[..., None]
    return dict(a=a, ah=ah, fa=fa, fi=fi, mf=mf, mi=mi, twr=twr, twi=twi, cr=cr, ci=ci)


def _dft_stage1(vgm, c):
    bt, _, _, d = vgm.shape
    a, ah = c["a"], c["ah"]
    return pl.pallas_call(
        functools.partial(_dft1_kernel, a=a),
        grid=(bt, DFT_G),
        in_specs=[
            pl.BlockSpec((None, None, 8 * ah, d), lambda n_, g: (n_, g, 0, 0)),
            pl.BlockSpec((16 * a, 8 * ah), lambda n_, g: (0, 0)),
            pl.BlockSpec((None, 8 * a, 1), lambda n_, g: (g, 0, 0)),
            pl.BlockSpec((None, 8 * a, 1), lambda n_, g: (g, 0, 0)),
        ],
        out_specs=pl.BlockSpec((None, None, 8 * a, d), lambda n_, g: (n_, g, 0, 0)),
        out_shape=jax.ShapeDtypeStruct((bt, DFT_G, 8 * a, d), I32),
        compiler_params=_cparams(("parallel", "parallel")),
        name="dft_stage1",
    )(vgm, c["fa"], c["twr"], c["twi"])


SPEC_F1 = 4


def _filter_spec_kernel(x_ref, mf_ref, s_ref, kr_ref, ki_ref, *, b):
    d = x_ref.shape[-1]
    for j in range(SPEC_F1):
        spec = []
        for side in range(2):
            xr, xi = _unpack_complex(x_ref[side, :, j].reshape(b, d))
            spec.append(jnp.dot(mf_ref[...], jnp.concatenate([xr, xi], axis=0), preferred_element_type=F32))
        fwd, bwd = spec
        kr_ref[j] = (fwd[:b] + bwd[:b]) * s_ref[...]
        ki_ref[j] = (fwd[b:] - bwd[b:]) * s_ref[...]


def _filter_spectrum(h_fwd, h_bwd0, scale, c):
    l, d = h_fwd.shape
    a, ah, b = c["a"], c["ah"], DFT_B
    hgm = jnp.stack([h_fwd, h_bwd0]).reshape(2, ah, DFT_G, 8, d).transpose(0, 2, 1, 3, 4).reshape(2, DFT_G, 8 * ah, d)
    x = _dft_stage1(hgm, c)
    out = jax.ShapeDtypeStruct((a, b, d), F32)
    return pl.pallas_call(
        functools.partial(_filter_spec_kernel, b=b),
        grid=(a // SPEC_F1,),
        in_specs=[
            pl.BlockSpec((2, DFT_G, SPEC_F1, 8, d), lambda i: (0, 0, i, 0, 0)),
            pl.BlockSpec((2 * b, 2 * b), lambda i: (0, 0)),
            pl.BlockSpec((1, d), lambda i: (0, 0)),
        ],
        out_specs=[pl.BlockSpec((SPEC_F1, b, d), lambda i: (i, 0, 0))] * 2,
        out_shape=[out, out],
        compiler_params=_cparams(("parallel",)),
        name="filter_spectrum",
    )(x.reshape(2, DFT_G, a, 8, d), c["mf"], scale.reshape(1, d))


def _long_conv(vgm, kr, ki, c):
    bt, _, _, d = vgm.shape
    a, ah, b = c["a"], c["ah"], DFT_B
    mf, mi, cr, ci, fi = c["mf"], c["mi"], c["cr"], c["ci"], c["fi"]
    x = _dft_stage1(vgm, c)

    at = SPEC_F1
    z = pl.pallas_call(
        functools.partial(_dft2_kernel, at=at, b=b),
        grid=(a // at, bt),
        in_specs=[
            pl.BlockSpec((None, DFT_G, at, 8, d), lambda i, n_: (n_, 0, i, 0, 0)),
            pl.BlockSpec((2 * b, 2 * b), lambda i, n_: (0, 0)),
            pl.BlockSpec((2 * b, 2 * b), lambda i, n_: (0, 0)),
            pl.BlockSpec((at, b, d), lambda i, n_: (i, 0, 0)),
            pl.BlockSpec((at, b, d), lambda i, n_: (i, 0, 0)),
            pl.BlockSpec((at, b, 1), lambda i, n_: (i, 0, 0)),
            pl.BlockSpec((at, b, 1), lambda i, n_: (i, 0, 0)),
        ],
        out_specs=pl.BlockSpec((None, DFT_G, at, 8, d), lambda i, n_: (n_, 0, i, 0, 0)),
        out_shape=jax.ShapeDtypeStruct((bt, DFT_G, a, 8, d), I32),
        compiler_params=_cparams(("parallel", "parallel")),
        name="dft_stage2",
    )(x.reshape(bt, DFT_G, a, 8, d), mf, mi, kr, ki, cr, ci)

    return pl.pallas_call(
        _dft3_kernel,
        grid=(bt, DFT_G),
        in_specs=[
            pl.BlockSpec((None, None, 8 * a, d), lambda n_, g: (n_, g, 0, 0)),
            pl.BlockSpec((8 * ah, 16 * a), lambda n_, g: (0, 0)),
        ],
        out_specs=pl.BlockSpec((None, None, 8 * ah, d), lambda n_, g: (n_, g, 0, 0)),
        out_shape=jax.ShapeDtypeStruct((bt, DFT_G, 8 * ah, d), F32),
        compiler_params=_cparams(("parallel", "parallel")),
        name="dft_stage3",
    )(z.reshape(bt, DFT_G, 8 * a, d), fi)


def _hyena_filter_taps(l, w_f0, b_f0, w_fin, b_fin, w_fout, sin_freq):
    hp = lax.Precision.HIGHEST
    d = w_fout.shape[-1] // 2
    t = jnp.linspace(0.0, 1.0, l, dtype=F32)[:, None]
    bands = (EMB_DIM - 1) // 2
    f = jnp.linspace(1e-4, bands - 1, bands, dtype=F32)[None, :]
    w = 2.0 * math.pi * jnp.arange(l, dtype=F32)[:, None] / l
    z = jnp.concatenate([t, jnp.cos(f * w), -jnp.sin(f * w)], axis=-1)
    hf = jnp.sin(sin_freq * (jnp.dot(z, w_f0, precision=hp) + b_f0))
    for i in range(N_INNER_MLPS):
        hf = jnp.sin(sin_freq * (jnp.dot(hf, w_fin[i], precision=hp) + b_fin[i]))
    hf = jnp.dot(hf, w_fout, precision=hp)
    max_decay = math.log(DECAY_TARGET) / FAST_DECAY_PCT
    min_decay = math.log(DECAY_TARGET) / SLOW_DECAY_PCT
    deltas = jnp.linspace(min_decay, max_decay, d, dtype=F32)[None, :]
    decay = jnp.exp(-t * jnp.abs(deltas))
    h_fwd = hf[:, :d] * decay
    lag = lax.broadcasted_iota(I32, (l, 1), 0)
    h_bwd0 = jnp.where(lag > 0, hf[:, d:] * decay, 0.0)
    norm = jnp.sum(jnp.abs(h_fwd), axis=0) + jnp.sum(jnp.abs(h_bwd0), axis=0)
    return h_fwd, h_bwd0, 1.0 / (norm * (2 * l))


def _hyena_tail_kernel(y_ref, v_ref, x0_ref, fb_ref, w_ref, b_ref, x_ref, o_ref, *, tm):
    y = _from_group_major(y_ref, tm)
    g = ((y + v_ref[...] * fb_ref[...]) * x0_ref[...]).astype(BF16)
    o_ref[...] = jnp.dot(g, w_ref[...], preferred_element_type=F32) + b_ref[...] + x_ref[...]


def _hyena_tail(ygm, v, x0, fft_bias, w_out, b_out, x):
    bt, l, d = x.shape
    tm = ROW_TILE
    tile = pl.BlockSpec((None, tm, d), lambda b, i: (b, i, 0))
    rowv = pl.BlockSpec((1, d), lambda b, i: (0, 0))
    return pl.pallas_call(
        functools.partial(_hyena_tail_kernel, tm=tm),
        grid=(bt, l // tm),
        in_specs=[pl.BlockSpec((None, DFT_G, 8 * tm // DFT_B, d), lambda b, i: (b, 0, i, 0)),
                  tile, tile, rowv, pl.BlockSpec((d, d), lambda b, i: (0, 0)), rowv, tile],
        out_specs=tile,
        out_shape=jax.ShapeDtypeStruct((bt, l, d), F32),
        compiler_params=_cparams(("parallel", "parallel")),
        name="hyena_tail",
    )(ygm, v, x0, fft_bias.reshape(1, d), w_out, b_out.reshape(1, d), x)


def _router_kernel(x_ref, g_ref, wt_ref, hn_ref, pt_ref):
    h = _rms(x_ref[...], g_ref[...])
    hn_ref[...] = h.astype(BF16)
    lt = lax.dot_general(wt_ref[...], h, (((1,), (1,)), ((), ())), precision=lax.Precision.HIGHEST,
                         preferred_element_type=F32)
    et = jnp.exp(lt - jnp.max(lt, axis=0, keepdims=True))
    pt_ref[...] = et / jnp.sum(et, axis=0, keepdims=True)


def _router(x, g, w_router):
    t, d = x.shape
    e = w_router.shape[1]
    return pl.pallas_call(
        _router_kernel,
        grid=(t // ROW_TILE,),
        in_specs=[
            pl.BlockSpec((ROW_TILE, d), lambda i: (i, 0)),
            pl.BlockSpec((1, d), lambda i: (0, 0)),
            pl.BlockSpec((e, d), lambda i: (0, 0)),
        ],
        out_specs=[
            pl.BlockSpec((ROW_TILE, d), lambda i: (i, 0)),
            pl.BlockSpec((e, ROW_TILE), lambda i: (0, i)),
        ],
        out_shape=[
            jax.ShapeDtypeStruct((t, d), BF16),
            jax.ShapeDtypeStruct((e, t), F32),
        ],
        compiler_params=_cparams(("parallel",)),
        name="router",
    )(x, g.reshape(1, d), w_router.T)


CUM_BLK = 256


def _select_kernel(p_ref, posm_ref, posx_ref, mask_ref, *, cap, t):
    keys = pltpu.bitcast(p_ref[...], I32)
    e = keys.shape[0]

    def vbit(i, tau):
        cand = tau | jnp.left_shift(jnp.int32(1), 30 - i)
        cnt = jnp.sum((keys >= cand).astype(I32), axis=1, keepdims=True)
        return jnp.where(cnt >= cap, cand, tau)

    tau = lax.fori_loop(0, 31, vbit, jnp.zeros((e, 1), I32))
    gt = keys > tau
    eq = keys == tau
    quota = cap - jnp.sum(gt.astype(I32), axis=1, keepdims=True)
    idx = lax.broadcasted_iota(I32, keys.shape, 1)
    nbits = max(1, (t - 1).bit_length())

    def ibit(i, jj):
        cand = jj | jnp.left_shift(jnp.int32(1), nbits - 1 - i)
        cnt = jnp.sum((eq & (idx < cand)).astype(I32), axis=1, keepdims=True)
        return jnp.where(cnt < quota, cand, jj)

    jlast = lax.fori_loop(0, nbits, ibit, jnp.zeros((e, 1), I32))
    mask = gt | (eq & (idx <= jlast))
    mask_ref[...] = mask.astype(F32)

    r = lax.broadcasted_iota(I32, (CUM_BLK, CUM_BLK), 0)
    c = lax.broadcasted_iota(I32, (CUM_BLK, CUM_BLK), 1)
    tri = (r <= c).astype(BF16)
    run = jnp.zeros((e, 1), F32)
    for blk in range(t // CUM_BLK):
        cols = pl.ds(blk * CUM_BLK, CUM_BLK)
        m = mask_ref[:, cols]
        incl = jnp.dot(m.astype(BF16), tri, preferred_element_type=F32)
        pos = (incl - m + run).astype(I32)
        posx_ref[:, cols] = pos
        posm_ref[:, cols] = jnp.where(m > 0.0, pos, -1)
        run = run + incl[:, CUM_BLK - 1:CUM_BLK]


def _select(probs_t, cap):
    e, t = probs_t.shape
    full = pl.BlockSpec((e, t), lambda: (0, 0))
    return pl.pallas_call(
        functools.partial(_select_kernel, cap=cap, t=t),
        in_specs=[full],
        out_specs=[full, full],
        out_shape=[jax.ShapeDtypeStruct((e, t), I32), jax.ShapeDtypeStruct((e, t), I32)],
        scratch_shapes=[pltpu.VMEM((e, t), F32)],
        compiler_params=pltpu.CompilerParams(vmem_limit_bytes=VMEM_LIMIT),
        name="select",
    )(probs_t)


GATHER_EXPERTS = 2


GATHER_ROWS = 64


def _gather_kernel(st_ref, nch_ref, hn_ref, posm_ref, p_ref, o_ref, g_ref, *, sub):
    eg = pl.program_id(0)
    k = pl.program_id(1)

    @pl.when(k == 0)
    def _():
        o_ref[...] = jnp.zeros_like(o_ref)
        g_ref[...] = jnp.zeros_like(g_ref)

    rows = lax.broadcasted_iota(I32, (GATHER_ROWS, GATHER_BLK), 0)
    for j in range(sub):
        blk = k * sub + j
        tok = pl.ds(j * GATHER_BLK, GATHER_BLK)

        def onehot(x, chunk):
            base = pl.multiple_of(st_ref[eg * GATHER_EXPERTS + x, blk], BF16_SUBLANE_TILE) + chunk * GATHER_ROWS
            hit = rows == posm_ref[x, :, tok] - base
            gate = jnp.sum(jnp.where(hit, p_ref[x, :, tok], 0.0), axis=1, keepdims=True)
            return base, hit.astype(BF16), gate

        def add_rows(x, base, slab, gate):
            win = pl.ds(pl.multiple_of(base, BF16_SUBLANE_TILE), GATHER_ROWS)
            o_ref[x, win, :] = o_ref[x, win, :] + slab.astype(BF16)
            g_ref[x, win, :] = g_ref[x, win, :] + gate

        first = [onehot(x, 0) for x in range(GATHER_EXPERTS)]
        slab = jnp.dot(jnp.concatenate([oh for _, oh, _ in first], axis=0), hn_ref[tok, :],
                       preferred_element_type=F32)
        for x in range(GATHER_EXPERTS):
            add_rows(x, first[x][0], slab[x * GATHER_ROWS:(x + 1) * GATHER_ROWS], first[x][2])

        for x in range(GATHER_EXPERTS):
            def extra(chunk, carry):
                base, oh, gate = onehot(x, chunk)
                add_rows(x, base, jnp.dot(oh, hn_ref[tok, :], preferred_element_type=F32), gate)
                return carry

            lax.fori_loop(1, nch_ref[eg * GATHER_EXPERTS + x, blk], extra, 0)


def _gather_tokens(hn, posm, probs_t, starts, nchunks, cap):
    t, d = hn.shape
    e = posm.shape[0]
    sub = 4
    nb = t // (GATHER_BLK * sub)
    cp = cap + GATHER_WIN
    return pl.pallas_call(
        functools.partial(_gather_kernel, sub=sub),
        grid_spec=pltpu.PrefetchScalarGridSpec(
            num_scalar_prefetch=2,
            grid=(e // GATHER_EXPERTS, nb),
            in_specs=[
                pl.BlockSpec((GATHER_BLK * sub, d), lambda ei, k, st, nc: (k, 0)),
                pl.BlockSpec((GATHER_EXPERTS, 1, GATHER_BLK * sub), lambda ei, k, st, nc: (ei, 0, k)),
                pl.BlockSpec((GATHER_EXPERTS, 1, GATHER_BLK * sub), lambda ei, k, st, nc: (ei, 0, k)),
            ],
            out_specs=[pl.BlockSpec((GATHER_EXPERTS, cp, d), lambda ei, k, st, nc: (ei, 0, 0)),
                       pl.BlockSpec((GATHER_EXPERTS, cp, LANE), lambda ei, k, st, nc: (ei, 0, 0))],
        ),
        out_shape=[jax.ShapeDtypeStruct((e, cp, d), BF16), jax.ShapeDtypeStruct((e, cp, LANE), F32)],
        compiler_params=_cparams(("arbitrary", "arbitrary")),
        name="gather_tokens",
    )(starts, nchunks, hn, posm.reshape(e, 1, t), probs_t.reshape(e, 1, t))


def _ffn_kernel(x_ref, gate_ref, wg_ref, wu_ref, wd_ref, o_ref, *, nrt, splits):
    r = pl.program_id(1)

    @pl.when(r < nrt)
    def _():
        x = x_ref[...]
        dn = None
        for c0, c1 in splits:
            g = jnp.dot(x, wg_ref[:, c0:c1], preferred_element_type=F32)
            u = jnp.dot(x, wu_ref[:, c0:c1], preferred_element_type=F32)
            h = (g * jax.nn.sigmoid(g) * u).astype(BF16)
            part = jnp.dot(h, wd_ref[c0:c1, :], preferred_element_type=F32)
            dn = part if dn is None else dn + part
        o_ref[...] = (dn * gate_ref[:, 0:1]).astype(BF16)

    @pl.when(r == nrt)
    def _():
        o_ref[...] = jnp.zeros_like(o_ref)


def _expert_ffn(xs, gates, w_gate, w_up, w_down, layer, cap):
    e, _, d = xs.shape
    f = w_gate.shape[-1]
    nrt = cap // FFN_ROWS
    half = (f // MXU_DIM + 1) // 2 * MXU_DIM
    splits = ((0, half), (half, f))
    row = lambda r: jnp.minimum(r, nrt - 1)
    return pl.pallas_call(
        functools.partial(_ffn_kernel, nrt=nrt, splits=splits),
        grid=(e, nrt + 1),
        in_specs=[
            pl.BlockSpec((None, FFN_ROWS, d), lambda ei, r: (ei, row(r), 0)),
            pl.BlockSpec((None, FFN_ROWS, LANE), lambda ei, r: (ei, row(r), 0)),
            pl.BlockSpec((None, None, d, f), lambda ei, r: (layer, ei, 0, 0)),
            pl.BlockSpec((None, None, d, f), lambda ei, r: (layer, ei, 0, 0)),
            pl.BlockSpec((None, None, f, d), lambda ei, r: (layer, ei, 0, 0)),
        ],
        out_specs=pl.BlockSpec((None, FFN_ROWS, d), lambda ei, r: (ei, r, 0)),
        out_shape=jax.ShapeDtypeStruct((e, cap + FFN_ROWS, d), BF16),
        compiler_params=_cparams(("parallel", "arbitrary")),
        name="expert_ffn",
    )(xs, gates, w_gate, w_up, w_down)


def _combine_kernel(st_ref, nch_ref, x_ref, posm_ref, g_ref, out_hbm, o_ref,
                    stack, lhs, xbuf, sem, xsem, *, ne, cp, final_norm):
    b = pl.program_id(0)
    nsteps = pl.num_programs(0)

    def window_copies(step, slot):
        for ei in range(ne):
            start = pl.multiple_of(ei * cp + st_ref[ei, step], BF16_SUBLANE_TILE)
            dst = stack.at[slot, pl.ds(ei * COMB_WIN, COMB_WIN), :]
            yield pltpu.make_async_copy(out_hbm.at[pl.ds(start, COMB_WIN), :], dst, sem.at[slot])

    @pl.when(b == 0)
    def _():
        for c in window_copies(0, 0):
            c.start()

    @pl.when(b + 1 < nsteps)
    def _():
        for c in window_copies(b + 1, (b + 1) % 2):
            c.start()

    cols = lax.broadcasted_iota(I32, (COMB_BLK, COMB_WIN), 1)

    def onehot(ei, chunk):
        rel = posm_ref[:, ei:ei + 1] - st_ref[ei, b] - chunk * COMB_WIN
        return (cols == rel).astype(BF16)

    for ei in range(ne):
        lhs[:, pl.ds(ei * COMB_WIN, COMB_WIN)] = onehot(ei, 0)

    slot = b % 2
    for c in window_copies(b, slot):
        c.wait()
    o_ref[...] = x_ref[...] + jnp.dot(lhs[...], stack[slot], preferred_element_type=F32)

    for ei in range(ne):
        def extra(chunk, carry):
            start = pl.multiple_of(ei * cp + st_ref[ei, b] + chunk * COMB_WIN, BF16_SUBLANE_TILE)
            cpy = pltpu.make_async_copy(out_hbm.at[pl.ds(start, COMB_WIN), :], xbuf, xsem)
            cpy.start()
            cpy.wait()
            o_ref[...] = o_ref[...] + jnp.dot(onehot(ei, chunk), xbuf[...], preferred_element_type=F32)
            return carry

        lax.fori_loop(1, nch_ref[ei, b], extra, 0)

    if final_norm:
        o_ref[...] = _rms(o_ref[...], g_ref[...])


def _combine(x, posm_tok, out_rows, starts, nchunks, final_g):
    t, d = x.shape
    e = posm_tok.shape[1]
    cp = out_rows.shape[0] // e
    final_norm = final_g is not None
    g = final_g if final_norm else jnp.ones((d,), F32)
    return pl.pallas_call(
        functools.partial(_combine_kernel, ne=e, cp=cp, final_norm=final_norm),
        grid_spec=pltpu.PrefetchScalarGridSpec(
            num_scalar_prefetch=2,
            grid=(t // COMB_BLK,),
            in_specs=[
                pl.BlockSpec((COMB_BLK, d), lambda i, st, nc: (i, 0)),
                pl.BlockSpec((COMB_BLK, e), lambda i, st, nc: (i, 0)),
                pl.BlockSpec((1, d), lambda i, st, nc: (0, 0)),
                pl.BlockSpec(memory_space=pl.ANY),
            ],
            out_specs=pl.BlockSpec((COMB_BLK, d), lambda i, st, nc: (i, 0)),
            scratch_shapes=[
                pltpu.VMEM((2, e * COMB_WIN, d), BF16),
                pltpu.VMEM((COMB_BLK, e * COMB_WIN), BF16),
                pltpu.VMEM((COMB_WIN, d), BF16),
                pltpu.SemaphoreType.DMA((2,)),
                pltpu.SemaphoreType.DMA,
            ],
        ),
        out_shape=jax.ShapeDtypeStruct((t, d), F32),
        compiler_params=_cparams(("arbitrary",)),
        name="combine",
    )(starts, nchunks, x, posm_tok, g.reshape(1, d), out_rows)


def _window_tables(posx, blk, win, cap):
    e = posx.shape[0]
    cum = posx[:, ::blk]
    cnt = jnp.concatenate([cum[:, 1:], jnp.full((e, 1), cap, I32)], axis=1) - cum
    starts = (cum // BF16_SUBLANE_TILE) * BF16_SUBLANE_TILE
    return starts, (cum - starts + cnt + win - 1) // win


def _ec_moe(x, g, w_router, w_gate, w_up, w_down, layer, final_g):
    t, d = x.shape
    e = w_router.shape[1]
    cap = max(1, min(t, CAPACITY_FACTOR * t // e))
    hn, probs_t = _router(x, g, w_router)
    posm, posx = _select(probs_t, cap)
    xs, gates = _gather_tokens(hn, posm, probs_t, *_window_tables(posx, GATHER_BLK, GATHER_ROWS, cap), cap)
    out_rows = _expert_ffn(xs, gates, w_gate, w_up, w_down, layer, cap)
    return _combine(x, posm.T, out_rows.reshape(-1, d), *_window_tables(posx, COMB_BLK, COMB_WIN, cap), final_g)


def _encoder(x3, p):
    bt, l, d = x3.shape
    t = bt * l
    x = x3.reshape(t, d)
    depth = p["norm_mix_g"].shape[0]
    for i in range(depth):
        j = i // 2
        if i % 2 == 0:
            u = _norm_glu(x, p["norm_mix_g"][i], p["conf_w_pw1"][j], p["conf_b_pw1"][j])
            x = _conf_tail(u.reshape(bt, l, d), x.reshape(bt, l, d), p["conf_w_dw"][j], p["conf_b_dw"][j],
                           p["conf_ln_g"][j], p["conf_ln_b"][j], p["conf_w_pw2"][j],
                           p["conf_b_pw2"][j]).reshape(t, d)
        else:
            u3 = _norm_proj(x, p["norm_mix_g"][i], p["hy_w_in"][j], p["hy_b_in"][j], tn=3 * d)
            vg, vgm, x0 = _short_conv(u3.reshape(bt, l, 3 * d), p["hy_w_short"][j], p["hy_b_short"][j])
            c = _dft_constants(l)
            taps = _hyena_filter_taps(l, p["hy_w_f0"][j], p["hy_b_f0"][j], p["hy_w_fin"][j],
                                      p["hy_b_fin"][j], p["hy_w_fout"][j], p["hy_sin_freq"][j])
            kr, ki = _filter_spectrum(*taps, c)
            ygm = _long_conv(vgm, kr, ki, c)
            x = _hyena_tail(ygm, vg, x0, p["hy_fft_bias"][j], p["hy_w_out"][j], p["hy_b_out"][j],
                            x.reshape(bt, l, d)).reshape(t, d)
        final_g = p["final_norm_g"] if i == depth - 1 else None
        x = _ec_moe(x, p["norm_ffn_g"][i], p["moe_w_router"][i], p["moe_w_gate"], p["moe_w_up"],
                    p["moe_w_down"], i, final_g)
    return x.reshape(bt, l, d)


_BF16_WEIGHTS = ("conf_w_pw1", "conf_w_pw2", "hy_w_in", "hy_w_out", "moe_w_gate", "moe_w_up", "moe_w_down")


def kernel(x_prompt, x_sample, norm_mix_g, norm_ffn_g, final_norm_g, conf_w_pw1, conf_b_pw1, conf_w_dw, conf_b_dw, conf_ln_g, conf_ln_b, conf_w_pw2, conf_b_pw2, hy_w_in, hy_b_in, hy_w_short, hy_b_short, hy_w_f0, hy_b_f0, hy_w_fin, hy_b_fin, hy_w_fout, hy_sin_freq, hy_fft_bias, hy_w_out, hy_b_out, moe_w_router, moe_w_gate, moe_w_up, moe_w_down):
    p = dict(norm_mix_g=norm_mix_g, norm_ffn_g=norm_ffn_g, final_norm_g=final_norm_g,
             conf_w_pw1=conf_w_pw1, conf_b_pw1=conf_b_pw1, conf_w_dw=conf_w_dw, conf_b_dw=conf_b_dw,
             conf_ln_g=conf_ln_g, conf_ln_b=conf_ln_b, conf_w_pw2=conf_w_pw2, conf_b_pw2=conf_b_pw2,
             hy_w_in=hy_w_in, hy_b_in=hy_b_in, hy_w_short=hy_w_short, hy_b_short=hy_b_short,
             hy_w_f0=hy_w_f0, hy_b_f0=hy_b_f0, hy_w_fin=hy_w_fin, hy_b_fin=hy_b_fin, hy_w_fout=hy_w_fout,
             hy_sin_freq=hy_sin_freq, hy_fft_bias=hy_fft_bias, hy_w_out=hy_w_out, hy_b_out=hy_b_out,
             moe_w_router=moe_w_router, moe_w_gate=moe_w_gate, moe_w_up=moe_w_up, moe_w_down=moe_w_down)
    for name in _BF16_WEIGHTS:
        p[name] = p[name].astype(BF16)
    return (_encoder(x_prompt, p), _encoder(x_sample, p))
```

```python
import functools
import math

import jax
import jax.numpy as jnp
from jax import lax
from jax.experimental import pallas as pl
from jax.experimental.pallas import tpu as pltpu

F32 = jnp.float32
BF16 = jnp.bfloat16
I32 = jnp.int32

RMS_EPS = 1e-6
LN_EPS = 1e-5
CONV_WIDTH = 31
SHORT_WIDTH = 3
EMB_DIM = 33
N_INNER_MLPS = 2
FAST_DECAY_PCT = 0.3
SLOW_DECAY_PCT = 1.5
DECAY_TARGET = 1e-2
N_EXPERTS = 16
CAPACITY_FACTOR = 2

LANE = 128
BF16_SUBLANE_TILE = 16
MXU_DIM = 256
VMEM_LIMIT = 56 * 1024 * 1024

ROW_TILE = 512
DFT_B = 128
GATHER_BLK = 256
GATHER_WIN = GATHER_BLK + BF16_SUBLANE_TILE
COMB_BLK = 512
COMB_WIN = 128
FFN_ROWS = 512


def _cparams(sem):
    return pltpu.CompilerParams(dimension_semantics=sem, vmem_limit_bytes=VMEM_LIMIT)


def _rms(x, g):
    return x * lax.rsqrt(jnp.mean(x * x, axis=-1, keepdims=True) + RMS_EPS) * g


def _norm_proj_kernel(x_ref, g_ref, w_ref, b_ref, o_ref):
    h = _rms(x_ref[...], g_ref[...]).astype(BF16)
    o_ref[...] = jnp.dot(h, w_ref[...], preferred_element_type=F32) + b_ref[...]


def _norm_glu_kernel(x_ref, g_ref, wa_ref, wg_ref, ba_ref, bg_ref, o_ref):
    h = _rms(x_ref[...], g_ref[...]).astype(BF16)
    a = jnp.dot(h, wa_ref[...], preferred_element_type=F32) + ba_ref[...]
    g = jnp.dot(h, wg_ref[...], preferred_element_type=F32) + bg_ref[...]
    o_ref[...] = a * jax.nn.sigmoid(g)


def _norm_proj(x, g, w, b, tn):
    t, d = x.shape
    n = w.shape[1]
    return pl.pallas_call(
        _norm_proj_kernel,
        grid=(t // ROW_TILE, n // tn),
        in_specs=[
            pl.BlockSpec((ROW_TILE, d), lambda i, j: (i, 0)),
            pl.BlockSpec((1, d), lambda i, j: (0, 0)),
            pl.BlockSpec((d, tn), lambda i, j: (0, j)),
            pl.BlockSpec((1, tn), lambda i, j: (0, j)),
        ],
        out_specs=pl.BlockSpec((ROW_TILE, tn), lambda i, j: (i, j)),
        out_shape=jax.ShapeDtypeStruct((t, n), F32),
        compiler_params=_cparams(("parallel", "parallel")),
        name="norm_proj",
    )(x, g.reshape(1, d), w, b.reshape(1, n))


def _norm_glu(x, g, w, b):
    t, d = x.shape
    n = w.shape[1] // 2
    return pl.pallas_call(
        _norm_glu_kernel,
        grid=(t // ROW_TILE,),
        in_specs=[
            pl.BlockSpec((ROW_TILE, d), lambda i: (i, 0)),
            pl.BlockSpec((1, d), lambda i: (0, 0)),
            pl.BlockSpec((d, n), lambda i: (0, 0)),
            pl.BlockSpec((d, n), lambda i: (0, 1)),
            pl.BlockSpec((1, n), lambda i: (0, 0)),
            pl.BlockSpec((1, n), lambda i: (0, 1)),
        ],
        out_specs=pl.BlockSpec((ROW_TILE, n), lambda i: (i, 0)),
        out_shape=jax.ShapeDtypeStruct((t, n), F32),
        compiler_params=_cparams(("parallel",)),
        name="norm_glu",
    )(x, g.reshape(1, d), w, w, b.reshape(1, 2 * n), b.reshape(1, 2 * n))


HALO = 16


CONV_CHUNK = 16


def _conf_tail_kernel(u_ref, up_ref, un_ref, x_ref, wdw_ref, bdw_ref, lng_ref, lnb_ref,
                      w2_ref, b2_ref, o_ref, ext_ref, acc_ref, *, tm, nblk):
    i = pl.program_id(1)
    d = u_ref.shape[-1]
    sl = d // LANE
    prev = up_ref[...].reshape(HALO, sl, LANE)
    nxt = un_ref[...].reshape(HALO, sl, LANE)
    ext_ref[pl.ds(0, HALO)] = jnp.where(i > 0, prev, jnp.zeros_like(prev))
    ext_ref[pl.ds(HALO, tm)] = u_ref[...].reshape(tm, sl, LANE)
    ext_ref[pl.ds(HALO + tm, HALO)] = jnp.where(i < nblk - 1, nxt, jnp.zeros_like(nxt))
    off = HALO - CONV_WIDTH // 2

    def chunk(c, carry):
        r0 = pl.multiple_of(c * CONV_CHUNK, CONV_CHUNK)
        acc = jnp.broadcast_to(bdw_ref[...], (CONV_CHUNK, sl, LANE))
        for k in range(CONV_WIDTH):
            acc = acc + ext_ref[pl.ds(r0 + off + k, CONV_CHUNK)] * wdw_ref[pl.ds(k, 1)]
        acc_ref[pl.ds(r0, CONV_CHUNK)] = acc
        return carry

    lax.fori_loop(0, tm // CONV_CHUNK, chunk, 0)
    acc = acc_ref[...].reshape(tm, d)
    mu = jnp.mean(acc, axis=-1, keepdims=True)
    xc = acc - mu
    var = jnp.mean(xc * xc, axis=-1, keepdims=True)
    y = xc * lax.rsqrt(var + LN_EPS) * lng_ref[...] + lnb_ref[...]
    y = (y * jax.nn.sigmoid(y)).astype(BF16)
    o_ref[...] = jnp.dot(y, w2_ref[...], preferred_element_type=F32) + b2_ref[...] + x_ref[...]


def _conf_tail(u, x, w_dw, b_dw, ln_g, ln_b, w2, b2):
    bt, l, d = u.shape
    tm = ROW_TILE
    nblk = l // tm
    hb = tm // HALO
    sl = d // LANE
    wpad = jnp.zeros((32, d), F32).at[:CONV_WIDTH].set(w_dw).reshape(32, sl, LANE)
    row = lambda a: a.reshape(1, d)
    return pl.pallas_call(
        functools.partial(_conf_tail_kernel, tm=tm, nblk=nblk),
        grid=(bt, nblk),
        in_specs=[
            pl.BlockSpec((None, tm, d), lambda b, i: (b, i, 0)),
            pl.BlockSpec((None, HALO, d), lambda b, i: (b, jnp.maximum(i * hb - 1, 0), 0)),
            pl.BlockSpec((None, HALO, d), lambda b, i: (b, jnp.minimum((i + 1) * hb, nblk * hb - 1), 0)),
            pl.BlockSpec((None, tm, d), lambda b, i: (b, i, 0)),
            pl.BlockSpec((32, sl, LANE), lambda b, i: (0, 0, 0)),
            pl.BlockSpec((1, sl, LANE), lambda b, i: (0, 0, 0)),
            pl.BlockSpec((1, d), lambda b, i: (0, 0)),
            pl.BlockSpec((1, d), lambda b, i: (0, 0)),
            pl.BlockSpec((d, d), lambda b, i: (0, 0)),
            pl.BlockSpec((1, d), lambda b, i: (0, 0)),
        ],
        out_specs=pl.BlockSpec((None, tm, d), lambda b, i: (b, i, 0)),
        out_shape=jax.ShapeDtypeStruct((bt, l, d), F32),
        scratch_shapes=[pltpu.VMEM((tm + 2 * HALO, sl, LANE), F32), pltpu.VMEM((tm, sl, LANE), F32)],
        compiler_params=_cparams(("parallel", "parallel")),
        name="conf_tail",
    )(u, u, u, x, wpad, b_dw.reshape(1, sl, LANE), row(ln_g), row(ln_b), w2, row(b2))


SHALO = 8


def _short_conv_kernel(u_ref, up_ref, un_ref, w_ref, b_ref, vg_ref, vgm_ref, x0_ref, ext_ref, *, tm, nblk, d):
    i = pl.program_id(1)
    prev = up_ref[...]
    nxt = un_ref[...]
    ext_ref[pl.ds(0, SHALO), :] = jnp.where(i > 0, prev, jnp.zeros_like(prev))
    ext_ref[pl.ds(SHALO, tm), :] = u_ref[...]
    ext_ref[pl.ds(SHALO + tm, SHALO), :] = jnp.where(i < nblk - 1, nxt, jnp.zeros_like(nxt))

    def conv(c):
        cols = pl.ds(c * d, d)
        acc = b_ref[:, cols]
        for k in range(SHORT_WIDTH):
            acc = acc + ext_ref[pl.ds(SHALO - 1 + k, tm), cols] * w_ref[pl.ds(k, 1), cols]
        return acc

    x0_ref[...] = conv(0)
    vg = conv(2) * conv(1)
    vg_ref[...] = vg
    _to_group_major(vg, vgm_ref, tm)


DFT_G = DFT_B // 8


def _to_group_major(val, ref, tm):
    for ai in range(tm // DFT_B):
        for g in range(DFT_G):
            ref[g, pl.ds(ai * 8, 8), :] = val[ai * DFT_B + g * 8: ai * DFT_B + g * 8 + 8, :]


def _from_group_major(ref, tm):
    return jnp.concatenate([ref[g, pl.ds(ai * 8, 8), :] for ai in range(tm // DFT_B) for g in range(DFT_G)],
                           axis=0)


def _short_conv(u3, w_short, b_short):
    bt, l, d3 = u3.shape
    d = d3 // 3
    tm = ROW_TILE
    nblk = l // tm
    hb = tm // SHALO
    gm_rows = 8 * tm // DFT_B
    wpad = jnp.zeros((8, d3), F32).at[:SHORT_WIDTH].set(w_short)
    out = jax.ShapeDtypeStruct((bt, l, d), F32)
    return pl.pallas_call(
        functools.partial(_short_conv_kernel, tm=tm, nblk=nblk, d=d),
        grid=(bt, nblk),
        in_specs=[
            pl.BlockSpec((None, tm, d3), lambda b, i: (b, i, 0)),
            pl.BlockSpec((None, SHALO, d3), lambda b, i: (b, jnp.maximum(i * hb - 1, 0), 0)),
            pl.BlockSpec((None, SHALO, d3), lambda b, i: (b, jnp.minimum((i + 1) * hb, nblk * hb - 1), 0)),
            pl.BlockSpec((8, d3), lambda b, i: (0, 0)),
            pl.BlockSpec((1, d3), lambda b, i: (0, 0)),
        ],
        out_specs=[
            pl.BlockSpec((None, tm, d), lambda b, i: (b, i, 0)),
            pl.BlockSpec((None, DFT_G, gm_rows, d), lambda b, i: (b, 0, i, 0)),
            pl.BlockSpec((None, tm, d), lambda b, i: (b, i, 0)),
        ],
        out_shape=[out, jax.ShapeDtypeStruct((bt, DFT_G, 8 * l // DFT_B, d), F32), out],
        scratch_shapes=[pltpu.VMEM((tm + 2 * SHALO, d3), F32)],
        compiler_params=_cparams(("parallel", "parallel")),
        name="short_conv",
    )(u3, u3, u3, wpad, b_short.reshape(1, d3))


_HI16 = -65536


def _pack_complex(re, im):
    hi = lax.bitcast_convert_type(re.astype(BF16).astype(F32), I32)
    lo = lax.bitcast_convert_type(im.astype(BF16).astype(F32), I32)
    return hi | lax.shift_right_logical(lo, jnp.int32(16))


def _unpack_complex(w):
    re = lax.bitcast_convert_type(w & jnp.int32(_HI16), F32)
    im = lax.bitcast_convert_type(lax.shift_left(w, jnp.int32(16)), F32)
    return re.astype(BF16), im.astype(BF16)


def _dft1_kernel(v_ref, fa_ref, x_ref, *, a):
    r = jnp.dot(fa_ref[...], v_ref[...].astype(BF16), preferred_element_type=F32)
    x_ref[...] = _pack_complex(r[:8 * a], r[8 * a:])


def _dft2_kernel(x_ref, mf_ref, mi_ref, kr_ref, ki_ref, z_ref, *, at, b):
    d = x_ref.shape[-1]
    for j in range(at):
        xr, xi = _unpack_complex(x_ref[:, j].reshape(b, d))
        v = jnp.dot(mf_ref[j], jnp.concatenate([xr, xi], axis=0), preferred_element_type=F32)
        vr, vi = v[:b], v[b:]
        kr, ki = kr_ref[j], ki_ref[j]
        yr = (vr * kr - vi * ki).astype(BF16)
        yi = (vr * ki + vi * kr).astype(BF16)
        z = jnp.dot(mi_ref[j], jnp.concatenate([yr, yi], axis=0), preferred_element_type=F32)
        z_ref[:, j] = _pack_complex(z[:b], z[b:]).reshape(DFT_G, 8, d)


def _dft3_kernel(z_ref, fi_ref, y_ref):
    zr, zi = _unpack_complex(z_ref[...])
    y_ref[...] = jnp.dot(fi_ref[...], jnp.concatenate([zr, zi], axis=0), preferred_element_type=F32)


def _dft_constants(l):
    n = 2 * l
    b = DFT_B
    a = n // b
    ah = a // 2
    two_pi = 2.0 * math.pi
    f1 = jnp.arange(a, dtype=F32)[:, None]
    aa = jnp.arange(ah, dtype=F32)[None, :]
    ang_a = two_pi * ((f1 * aa) % a) / a
    eye8 = jnp.eye(8, dtype=F32)
    fa = jnp.kron(jnp.concatenate([jnp.cos(ang_a), -jnp.sin(ang_a)], axis=0), eye8).astype(BF16)
    fi = jnp.kron(jnp.concatenate([jnp.cos(ang_a).T, -jnp.sin(ang_a).T], axis=1), eye8).astype(BF16)
    bb = jnp.arange(b, dtype=F32)
    ang_b = two_pi * ((bb[:, None] * bb[None, :]) % b) / b
    cb, sb = jnp.cos(ang_b)[None], jnp.sin(ang_b)[None]
    ang_t = two_pi * (jnp.arange(a, dtype=F32)[:, None] * bb[None, :]) / n
    ct, st = jnp.cos(ang_t), jnp.sin(ang_t)
    block = lambda re, im: jnp.concatenate([jnp.concatenate([re, -im], 2), jnp.concatenate([im, re], 2)], 1)
    tc, ts = ct[:, None, :], st[:, None, :]
    mf = block(cb * tc - sb * ts, -(cb * ts + sb * tc)).astype(BF16)
    rc, rs = ct[:, :, None], st[:, :, None]
    mi = block(rc * cb - rs * sb, rc * sb + rs * cb).astype(BF16)
    return dict(a=a, ah=ah, fa=fa, fi=fi, mf=mf, mi=mi)


def _dft_stage1(vgm, c):
    bt, _, _, d = vgm.shape
    a, ah = c["a"], c["ah"]
    return pl.pallas_call(
        functools.partial(_dft1_kernel, a=a),
        grid=(bt, DFT_G),
        in_specs=[
            pl.BlockSpec((None, None, 8 * ah, d), lambda n_, g: (n_, g, 0, 0)),
            pl.BlockSpec((16 * a, 8 * ah), lambda n_, g: (0, 0)),
        ],
        out_specs=pl.BlockSpec((None, None, 8 * a, d), lambda n_, g: (n_, g, 0, 0)),
        out_shape=jax.ShapeDtypeStruct((bt, DFT_G, 8 * a, d), I32),
        compiler_params=_cparams(("parallel", "parallel")),
        name="dft_stage1",
    )(vgm, c["fa"])


SPEC_F1 = 4


def _filter_spec_kernel(x_ref, mf_ref, s_ref, kr_ref, ki_ref, *, b):
    d = x_ref.shape[-1]
    for j in range(SPEC_F1):
        spec = []
        for side in range(2):
            xr, xi = _unpack_complex(x_ref[side, :, j].reshape(b, d))
            spec.append(jnp.dot(mf_ref[j], jnp.concatenate([xr, xi], axis=0), preferred_element_type=F32))
        fwd, bwd = spec
        kr_ref[j] = (fwd[:b] + bwd[:b]) * s_ref[...]
        ki_ref[j] = (fwd[b:] - bwd[b:]) * s_ref[...]


def _filter_spectrum(hgm, scale, c):
    d = hgm.shape[-1]
    a, b = c["a"], DFT_B
    x = _dft_stage1(hgm, c)
    out = jax.ShapeDtypeStruct((a, b, d), F32)
    return pl.pallas_call(
        functools.partial(_filter_spec_kernel, b=b),
        grid=(a // SPEC_F1,),
        in_specs=[
            pl.BlockSpec((2, DFT_G, SPEC_F1, 8, d), lambda i: (0, 0, i, 0, 0)),
            pl.BlockSpec((SPEC_F1, 2 * b, 2 * b), lambda i: (i, 0, 0)),
            pl.BlockSpec((1, d), lambda i: (0, 0)),
        ],
        out_specs=[pl.BlockSpec((SPEC_F1, b, d), lambda i: (i, 0, 0))] * 2,
        out_shape=[out, out],
        compiler_params=_cparams(("parallel",)),
        name="filter_spectrum",
    )(x.reshape(2, DFT_G, a, 8, d), c["mf"], scale.reshape(1, d))


def _long_conv(vgm, kr, ki, c):
    bt, _, _, d = vgm.shape
    a, ah, b = c["a"], c["ah"], DFT_B
    mf, mi, fi = c["mf"], c["mi"], c["fi"]
    x = _dft_stage1(vgm, c)

    at = SPEC_F1
    z = pl.pallas_call(
        functools.partial(_dft2_kernel, at=at, b=b),
        grid=(a // at, bt),
        in_specs=[
            pl.BlockSpec((None, DFT_G, at, 8, d), lambda i, n_: (n_, 0, i, 0, 0)),
            pl.BlockSpec((at, 2 * b, 2 * b), lambda i, n_: (i, 0, 0)),
            pl.BlockSpec((at, 2 * b, 2 * b), lambda i, n_: (i, 0, 0)),
            pl.BlockSpec((at, b, d), lambda i, n_: (i, 0, 0)),
            pl.BlockSpec((at, b, d), lambda i, n_: (i, 0, 0)),
        ],
        out_specs=pl.BlockSpec((None, DFT_G, at, 8, d), lambda i, n_: (n_, 0, i, 0, 0)),
        out_shape=jax.ShapeDtypeStruct((bt, DFT_G, a, 8, d), I32),
        compiler_params=_cparams(("parallel", "parallel")),
        name="dft_stage2",
    )(x.reshape(bt, DFT_G, a, 8, d), mf, mi, kr, ki)

    return pl.pallas_call(
        _dft3_kernel,
        grid=(bt, DFT_G),
        in_specs=[
            pl.BlockSpec((None, None, 8 * a, d), lambda n_, g: (n_, g, 0, 0)),
            pl.BlockSpec((8 * ah, 16 * a), lambda n_, g: (0, 0)),
        ],
        out_specs=pl.BlockSpec((None, None, 8 * ah, d), lambda n_, g: (n_, g, 0, 0)),
        out_shape=jax.ShapeDtypeStruct((bt, DFT_G, 8 * ah, d), F32),
        compiler_params=_cparams(("parallel", "parallel")),
        name="dft_stage3",
    )(z.reshape(bt, DFT_G, 8 * a, d), fi)


def _filter_taps_kernel(fv_ref, w0_ref, b0_ref, win_ref, bin_ref, wout_ref, freq_ref, dl_ref,
                        h_ref, s_ref, *, tm, l, d):
    hp = lax.Precision.HIGHEST
    n = (pl.program_id(0) * tm + lax.broadcasted_iota(I32, (tm, 1), 0)).astype(F32)
    t = n / (l - 1)
    arg = (2.0 * math.pi / l) * n * fv_ref[...]
    lane = lax.broadcasted_iota(I32, (tm, LANE), 1)
    bands = (EMB_DIM - 1) // 2
    z = jnp.where(lane == 0, t, jnp.where(lane <= bands, jnp.cos(arg),
                                          jnp.where(lane <= 2 * bands, -jnp.sin(arg), 0.0)))
    freq = freq_ref[...]
    hf = jnp.sin(freq * (jnp.dot(z, w0_ref[...], precision=hp, preferred_element_type=F32) + b0_ref[...]))
    for i in range(N_INNER_MLPS):
        hf = jnp.sin(freq * (jnp.dot(hf, win_ref[i], precision=hp, preferred_element_type=F32) + bin_ref[i]))
    hf = jnp.dot(hf, wout_ref[...], precision=hp, preferred_element_type=F32)
    decay = jnp.exp(-t * jnp.abs(dl_ref[...]))
    h_fwd = hf[:, :d] * decay
    h_bwd0 = jnp.where(n > 0.0, hf[:, d:] * decay, 0.0)
    _to_group_major(h_fwd, h_ref.at[0], tm)
    _to_group_major(h_bwd0, h_ref.at[1], tm)
    part = jnp.sum(jnp.abs(h_fwd) + jnp.abs(h_bwd0), axis=0, keepdims=True)
    s_ref[...] = jnp.broadcast_to(part, s_ref.shape)


def _hyena_filter_taps(l, w_f0, b_f0, w_fin, b_fin, w_fout, sin_freq):
    d = w_fout.shape[-1] // 2
    o = w_f0.shape[1]
    tm = ROW_TILE
    nblk = l // tm
    bands = (EMB_DIM - 1) // 2
    f = jnp.linspace(1e-4, bands - 1, bands, dtype=F32)
    fvec = jnp.zeros((1, LANE), F32).at[0, 1:1 + bands].set(f).at[0, 1 + bands:1 + 2 * bands].set(f)
    pad = lambda a, shape: jnp.zeros(shape, F32).at[tuple(slice(0, s) for s in a.shape)].set(a)
    max_decay = math.log(DECAY_TARGET) / FAST_DECAY_PCT
    min_decay = math.log(DECAY_TARGET) / SLOW_DECAY_PCT
    deltas = jnp.linspace(min_decay, max_decay, d, dtype=F32)[None, :]
    full = lambda *shape: pl.BlockSpec(shape, lambda i: (0,) * len(shape))
    hgm, sums = pl.pallas_call(
        functools.partial(_filter_taps_kernel, tm=tm, l=l, d=d),
        grid=(nblk,),
        in_specs=[full(1, LANE), full(LANE, LANE), full(1, LANE), full(N_INNER_MLPS, LANE, LANE),
                  full(N_INNER_MLPS, 1, LANE), full(LANE, 2 * d), full(1, LANE), full(1, d)],
        out_specs=[pl.BlockSpec((2, DFT_G, 8 * tm // DFT_B, d), lambda i: (0, 0, i, 0)),
                   pl.BlockSpec((None, 8, d), lambda i: (i, 0, 0))],
        out_shape=[jax.ShapeDtypeStruct((2, DFT_G, 8 * l // DFT_B, d), F32),
                   jax.ShapeDtypeStruct((nblk, 8, d), F32)],
        compiler_params=_cparams(("parallel",)),
        name="filter_taps",
    )(fvec, pad(w_f0, (LANE, LANE)), pad(b_f0[None], (1, LANE)), pad(w_fin, (N_INNER_MLPS, LANE, LANE)),
      pad(b_fin[:, None], (N_INNER_MLPS, 1, LANE)), pad(w_fout, (LANE, 2 * d)), pad(sin_freq[None], (1, LANE)), deltas)
    assert o <= LANE and EMB_DIM <= LANE
    return hgm, 1.0 / (jnp.sum(sums[:, 0, :], axis=0) * (2 * l))


def _hyena_tail_kernel(y_ref, v_ref, x0_ref, fb_ref, w_ref, b_ref, x_ref, o_ref, *, tm):
    y = _from_group_major(y_ref, tm)
    g = ((y + v_ref[...] * fb_ref[...]) * x0_ref[...]).astype(BF16)
    o_ref[...] = jnp.dot(g, w_ref[...], preferred_element_type=F32) + b_ref[...] + x_ref[...]


def _hyena_tail(ygm, v, x0, fft_bias, w_out, b_out, x):
    bt, l, d = x.shape
    tm = ROW_TILE
    tile = pl.BlockSpec((None, tm, d), lambda b, i: (b, i, 0))
    rowv = pl.BlockSpec((1, d), lambda b, i: (0, 0))
    return pl.pallas_call(
        functools.partial(_hyena_tail_kernel, tm=tm),
        grid=(bt, l // tm),
        in_specs=[pl.BlockSpec((None, DFT_G, 8 * tm // DFT_B, d), lambda b, i: (b, 0, i, 0)),
                  tile, tile, rowv, pl.BlockSpec((d, d), lambda b, i: (0, 0)), rowv, tile],
        out_specs=tile,
        out_shape=jax.ShapeDtypeStruct((bt, l, d), F32),
        compiler_params=_cparams(("parallel", "parallel")),
        name="hyena_tail",
    )(ygm, v, x0, fft_bias.reshape(1, d), w_out, b_out.reshape(1, d), x)


def _router_kernel(x_ref, g_ref, wt_ref, hn_ref, pt_ref):
    h = _rms(x_ref[...], g_ref[...])
    hn_ref[...] = h.astype(BF16)
    lt = lax.dot_general(wt_ref[...], h, (((1,), (1,)), ((), ())), precision=lax.Precision.HIGHEST,
                         preferred_element_type=F32)
    et = jnp.exp(lt - jnp.max(lt, axis=0, keepdims=True))
    pt_ref[...] = et / jnp.sum(et, axis=0, keepdims=True)


def _router(x, g, w_router):
    t, d = x.shape
    e = w_router.shape[1]
    return pl.pallas_call(
        _router_kernel,
        grid=(t // ROW_TILE,),
        in_specs=[
            pl.BlockSpec((ROW_TILE, d), lambda i: (i, 0)),
            pl.BlockSpec((1, d), lambda i: (0, 0)),
            pl.BlockSpec((e, d), lambda i: (0, 0)),
        ],
        out_specs=[
            pl.BlockSpec((ROW_TILE, d), lambda i: (i, 0)),
            pl.BlockSpec((e, ROW_TILE), lambda i: (0, i)),
        ],
        out_shape=[
            jax.ShapeDtypeStruct((t, d), BF16),
            jax.ShapeDtypeStruct((e, t), F32),
        ],
        compiler_params=_cparams(("parallel",)),
        name="router",
    )(x, g.reshape(1, d), w_router.T)


CUM_BLK = 256


def _select_kernel(p_ref, posm_ref, posx_ref, mask_ref, *, cap, t):
    keys = pltpu.bitcast(p_ref[...], I32)
    e = keys.shape[0]

    def vbit(i, tau):
        cand = tau | jnp.left_shift(jnp.int32(1), 30 - i)
        cnt = jnp.sum((keys >= cand).astype(I32), axis=1, keepdims=True)
        return jnp.where(cnt >= cap, cand, tau)

    tau = lax.fori_loop(0, 31, vbit, jnp.zeros((e, 1), I32))
    gt = keys > tau
    eq = keys == tau
    quota = cap - jnp.sum(gt.astype(I32), axis=1, keepdims=True)
    idx = lax.broadcasted_iota(I32, keys.shape, 1)
    nbits = max(1, (t - 1).bit_length())

    def ibit(i, jj):
        cand = jj | jnp.left_shift(jnp.int32(1), nbits - 1 - i)
        cnt = jnp.sum((eq & (idx < cand)).astype(I32), axis=1, keepdims=True)
        return jnp.where(cnt < quota, cand, jj)

    jlast = lax.fori_loop(0, nbits, ibit, jnp.zeros((e, 1), I32))
    mask = gt | (eq & (idx <= jlast))
    mask_ref[...] = mask.astype(F32)

    r = lax.broadcasted_iota(I32, (CUM_BLK, CUM_BLK), 0)
    c = lax.broadcasted_iota(I32, (CUM_BLK, CUM_BLK), 1)
    tri = (r <= c).astype(BF16)
    run = jnp.zeros((e, 1), F32)
    for blk in range(t // CUM_BLK):
        cols = pl.ds(blk * CUM_BLK, CUM_BLK)
        m = mask_ref[:, cols]
        incl = jnp.dot(m.astype(BF16), tri, preferred_element_type=F32)
        pos = (incl - m + run).astype(I32)
        posx_ref[:, cols] = pos
        posm_ref[:, cols] = jnp.where(m > 0.0, pos, -1)
        run = run + incl[:, CUM_BLK - 1:CUM_BLK]


def _select(probs_t, cap):
    e, t = probs_t.shape
    full = pl.BlockSpec((e, t), lambda: (0, 0))
    return pl.pallas_call(
        functools.partial(_select_kernel, cap=cap, t=t),
        in_specs=[full],
        out_specs=[full, full],
        out_shape=[jax.ShapeDtypeStruct((e, t), I32), jax.ShapeDtypeStruct((e, t), I32)],
        scratch_shapes=[pltpu.VMEM((e, t), F32)],
        compiler_params=pltpu.CompilerParams(vmem_limit_bytes=VMEM_LIMIT),
        name="select",
    )(probs_t)


GATHER_EXPERTS = 2


GATHER_ROWS = 64


def _gather_kernel(st_ref, nch_ref, hn_ref, posm_ref, p_ref, o_ref, g_ref, *, sub):
    eg = pl.program_id(0)
    k = pl.program_id(1)

    @pl.when(k == 0)
    def _():
        o_ref[...] = jnp.zeros_like(o_ref)
        g_ref[...] = jnp.zeros_like(g_ref)

    rows = lax.broadcasted_iota(I32, (GATHER_ROWS, GATHER_BLK), 0)
    for j in range(sub):
        blk = k * sub + j
        tok = pl.ds(j * GATHER_BLK, GATHER_BLK)

        def onehot(x, chunk):
            base = pl.multiple_of(st_ref[eg * GATHER_EXPERTS + x, blk], BF16_SUBLANE_TILE) + chunk * GATHER_ROWS
            hit = rows == posm_ref[x, :, tok] - base
            gate = jnp.sum(jnp.where(hit, p_ref[x, :, tok], 0.0), axis=1, keepdims=True)
            return base, hit.astype(BF16), gate

        def add_rows(x, base, slab, gate):
            win = pl.ds(pl.multiple_of(base, BF16_SUBLANE_TILE), GATHER_ROWS)
            o_ref[x, win, :] = o_ref[x, win, :] + slab.astype(BF16)
            g_ref[x, win, :] = g_ref[x, win, :] + gate

        first = [onehot(x, 0) for x in range(GATHER_EXPERTS)]
        slab = jnp.dot(jnp.concatenate([oh for _, oh, _ in first], axis=0), hn_ref[tok, :],
                       preferred_element_type=F32)
        for x in range(GATHER_EXPERTS):
            add_rows(x, first[x][0], slab[x * GATHER_ROWS:(x + 1) * GATHER_ROWS], first[x][2])

        for x in range(GATHER_EXPERTS):
            def extra(chunk, carry):
                base, oh, gate = onehot(x, chunk)
                add_rows(x, base, jnp.dot(oh, hn_ref[tok, :], preferred_element_type=F32), gate)
                return carry

            lax.fori_loop(1, nch_ref[eg * GATHER_EXPERTS + x, blk], extra, 0)


def _gather_tokens(hn, posm, probs_t, starts, nchunks, cap):
    t, d = hn.shape
    e = posm.shape[0]
    sub = 4
    nb = t // (GATHER_BLK * sub)
    cp = cap + GATHER_WIN
    return pl.pallas_call(
        functools.partial(_gather_kernel, sub=sub),
        grid_spec=pltpu.PrefetchScalarGridSpec(
            num_scalar_prefetch=2,
            grid=(e // GATHER_EXPERTS, nb),
            in_specs=[
                pl.BlockSpec((GATHER_BLK * sub, d), lambda ei, k, st, nc: (k, 0)),
                pl.BlockSpec((GATHER_EXPERTS, 1, GATHER_BLK * sub), lambda ei, k, st, nc: (ei, 0, k)),
                pl.BlockSpec((GATHER_EXPERTS, 1, GATHER_BLK * sub), lambda ei, k, st, nc: (ei, 0, k)),
            ],
            out_specs=[pl.BlockSpec((GATHER_EXPERTS, cp, d), lambda ei, k, st, nc: (ei, 0, 0)),
                       pl.BlockSpec((GATHER_EXPERTS, cp, LANE), lambda ei, k, st, nc: (ei, 0, 0))],
        ),
        out_shape=[jax.ShapeDtypeStruct((e, cp, d), BF16), jax.ShapeDtypeStruct((e, cp, LANE), F32)],
        compiler_params=_cparams(("arbitrary", "arbitrary")),
        name="gather_tokens",
    )(starts, nchunks, hn, posm.reshape(e, 1, t), probs_t.reshape(e, 1, t))


def _ffn_kernel(x_ref, gate_ref, wg_ref, wu_ref, wd_ref, o_ref, *, nrt, splits):
    r = pl.program_id(1)

    @pl.when(r < nrt)
    def _():
        x = x_ref[...]
        dn = None
        for c0, c1 in splits:
            g = jnp.dot(x, wg_ref[:, c0:c1], preferred_element_type=F32)
            u = jnp.dot(x, wu_ref[:, c0:c1], preferred_element_type=F32)
            h = (g * jax.nn.sigmoid(g) * u).astype(BF16)
            part = jnp.dot(h, wd_ref[c0:c1, :], preferred_element_type=F32)
            dn = part if dn is None else dn + part
        o_ref[...] = (dn * gate_ref[:, 0:1]).astype(BF16)

    @pl.when(r == nrt)
    def _():
        o_ref[...] = jnp.zeros_like(o_ref)


def _expert_ffn(xs, gates, w_gate, w_up, w_down, layer, cap):
    e, _, d = xs.shape
    f = w_gate.shape[-1]
    nrt = cap // FFN_ROWS
    half = (f // MXU_DIM + 1) // 2 * MXU_DIM
    splits = ((0, half), (half, f))
    row = lambda r: jnp.minimum(r, nrt - 1)
    return pl.pallas_call(
        functools.partial(_ffn_kernel, nrt=nrt, splits=splits),
        grid=(e, nrt + 1),
        in_specs=[
            pl.BlockSpec((None, FFN_ROWS, d), lambda ei, r: (ei, row(r), 0)),
            pl.BlockSpec((None, FFN_ROWS, LANE), lambda ei, r: (ei, row(r), 0)),
            pl.BlockSpec((None, None, d, f), lambda ei, r: (layer, ei, 0, 0)),
            pl.BlockSpec((None, None, d, f), lambda ei, r: (layer, ei, 0, 0)),
            pl.BlockSpec((None, None, f, d), lambda ei, r: (layer, ei, 0, 0)),
        ],
        out_specs=pl.BlockSpec((None, FFN_ROWS, d), lambda ei, r: (ei, r, 0)),
        out_shape=jax.ShapeDtypeStruct((e, cap + FFN_ROWS, d), BF16),
        compiler_params=_cparams(("parallel", "arbitrary")),
        name="expert_ffn",
    )(xs, gates, w_gate, w_up, w_down)


def _combine_kernel(st_ref, nch_ref, x_ref, posm_ref, g_ref, out_hbm, o_ref,
                    stack, lhs, xbuf, sem, xsem, *, ne, cp, final_norm):
    b = pl.program_id(0)
    nsteps = pl.num_programs(0)

    def window_copies(step, slot):
        for ei in range(ne):
            start = pl.multiple_of(ei * cp + st_ref[ei, step], BF16_SUBLANE_TILE)
            dst = stack.at[slot, pl.ds(ei * COMB_WIN, COMB_WIN), :]
            yield pltpu.make_async_copy(out_hbm.at[pl.ds(start, COMB_WIN), :], dst, sem.at[slot])

    @pl.when(b == 0)
    def _():
        for c in window_copies(0, 0):
            c.start()

    @pl.when(b + 1 < nsteps)
    def _():
        for c in window_copies(b + 1, (b + 1) % 2):
            c.start()

    cols = lax.broadcasted_iota(I32, (COMB_BLK, COMB_WIN), 1)

    def onehot(ei, chunk):
        rel = posm_ref[:, ei:ei + 1] - st_ref[ei, b] - chunk * COMB_WIN
        return (cols == rel).astype(BF16)

    for ei in range(ne):
        lhs[:, pl.ds(ei * COMB_WIN, COMB_WIN)] = onehot(ei, 0)

    slot = b % 2
    for c in window_copies(b, slot):
        c.wait()
    o_ref[...] = x_ref[...] + jnp.dot(lhs[...], stack[slot], preferred_element_type=F32)

    for ei in range(ne):
        def extra(chunk, carry):
            start = pl.multiple_of(ei * cp + st_ref[ei, b] + chunk * COMB_WIN, BF16_SUBLANE_TILE)
            cpy = pltpu.make_async_copy(out_hbm.at[pl.ds(start, COMB_WIN), :], xbuf, xsem)
            cpy.start()
            cpy.wait()
            o_ref[...] = o_ref[...] + jnp.dot(onehot(ei, chunk), xbuf[...], preferred_element_type=F32)
            return carry

        lax.fori_loop(1, nch_ref[ei, b], extra, 0)

    if final_norm:
        o_ref[...] = _rms(o_ref[...], g_ref[...])


def _combine(x, posm_tok, out_rows, starts, nchunks, final_g):
    t, d = x.shape
    e = posm_tok.shape[1]
    cp = out_rows.shape[0] // e
    final_norm = final_g is not None
    g = final_g if final_norm else jnp.ones((d,), F32)
    return pl.pallas_call(
        functools.partial(_combine_kernel, ne=e, cp=cp, final_norm=final_norm),
        grid_spec=pltpu.PrefetchScalarGridSpec(
            num_scalar_prefetch=2,
            grid=(t // COMB_BLK,),
            in_specs=[
                pl.BlockSpec((COMB_BLK, d), lambda i, st, nc: (i, 0)),
                pl.BlockSpec((COMB_BLK, e), lambda i, st, nc: (i, 0)),
                pl.BlockSpec((1, d), lambda i, st, nc: (0, 0)),
                pl.BlockSpec(memory_space=pl.ANY),
            ],
            out_specs=pl.BlockSpec((COMB_BLK, d), lambda i, st, nc: (i, 0)),
            scratch_shapes=[
                pltpu.VMEM((2, e * COMB_WIN, d), BF16),
                pltpu.VMEM((COMB_BLK, e * COMB_WIN), BF16),
                pltpu.VMEM((COMB_WIN, d), BF16),
                pltpu.SemaphoreType.DMA((2,)),
                pltpu.SemaphoreType.DMA,
            ],
        ),
        out_shape=jax.ShapeDtypeStruct((t, d), F32),
        compiler_params=_cparams(("arbitrary",)),
        name="combine",
    )(starts, nchunks, x, posm_tok, g.reshape(1, d), out_rows)


def _window_tables(posx, blk, win, cap):
    e = posx.shape[0]
    cum = posx[:, ::blk]
    cnt = jnp.concatenate([cum[:, 1:], jnp.full((e, 1), cap, I32)], axis=1) - cum
    starts = (cum // BF16_SUBLANE_TILE) * BF16_SUBLANE_TILE
    return starts, (cum - starts + cnt + win - 1) // win


def _ec_moe(x, g, w_router, w_gate, w_up, w_down, layer, final_g):
    t, d = x.shape
    e = w_router.shape[1]
    cap = max(1, min(t, CAPACITY_FACTOR * t // e))
    hn, probs_t = _router(x, g, w_router)
    posm, posx = _select(probs_t, cap)
    xs, gates = _gather_tokens(hn, posm, probs_t, *_window_tables(posx, GATHER_BLK, GATHER_ROWS, cap), cap)
    out_rows = _expert_ffn(xs, gates, w_gate, w_up, w_down, layer, cap)
    return _combine(x, posm.T, out_rows.reshape(-1, d), *_window_tables(posx, COMB_BLK, COMB_WIN, cap), final_g)


def _encoder(x3, p):
    bt, l, d = x3.shape
    t = bt * l
    x = x3.reshape(t, d)
    depth = p["norm_mix_g"].shape[0]
    for i in range(depth):
        j = i // 2
        if i % 2 == 0:
            u = _norm_glu(x, p["norm_mix_g"][i], p["conf_w_pw1"][j], p["conf_b_pw1"][j])
            x = _conf_tail(u.reshape(bt, l, d), x.reshape(bt, l, d), p["conf_w_dw"][j], p["conf_b_dw"][j],
                           p["conf_ln_g"][j], p["conf_ln_b"][j], p["conf_w_pw2"][j],
                           p["conf_b_pw2"][j]).reshape(t, d)
        else:
            u3 = _norm_proj(x, p["norm_mix_g"][i], p["hy_w_in"][j], p["hy_b_in"][j], tn=3 * d)
            vg, vgm, x0 = _short_conv(u3.reshape(bt, l, 3 * d), p["hy_w_short"][j], p["hy_b_short"][j])
            c = _dft_constants(l)
            taps = _hyena_filter_taps(l, p["hy_w_f0"][j], p["hy_b_f0"][j], p["hy_w_fin"][j],
                                      p["hy_b_fin"][j], p["hy_w_fout"][j], p["hy_sin_freq"][j])
            kr, ki = _filter_spectrum(*taps, c)
            ygm = _long_conv(vgm, kr, ki, c)
            x = _hyena_tail(ygm, vg, x0, p["hy_fft_bias"][j], p["hy_w_out"][j], p["hy_b_out"][j],
                            x.reshape(bt, l, d)).reshape(t, d)
        final_g = p["final_norm_g"] if i == depth - 1 else None
        x = _ec_moe(x, p["norm_ffn_g"][i], p["moe_w_router"][i], p["moe_w_gate"], p["moe_w_up"],
                    p["moe_w_down"], i, final_g)
    return x.reshape(bt, l, d)


_BF16_WEIGHTS = ("conf_w_pw1", "conf_w_pw2", "hy_w_in", "hy_w_out", "moe_w_gate", "moe_w_up", "moe_w_down")


def kernel(x_prompt, x_sample, norm_mix_g, norm_ffn_g, final_norm_g, conf_w_pw1, conf_b_pw1, conf_w_dw, conf_b_dw, conf_ln_g, conf_ln_b, conf_w_pw2, conf_b_pw2, hy_w_in, hy_b_in, hy_w_short, hy_b_short, hy_w_f0, hy_b_f0, hy_w_fin, hy_b_fin, hy_w_fout, hy_sin_freq, hy_fft_bias, hy_w_out, hy_b_out, moe_w_router, moe_w_gate, moe_w_up, moe_w_down):
    p = dict(norm_mix_g=norm_mix_g, norm_ffn_g=norm_ffn_g, final_norm_g=final_norm_g,
             conf_w_pw1=conf_w_pw1, conf_b_pw1=conf_b_pw1, conf_w_dw=conf_w_dw, conf_b_dw=conf_b_dw,
             conf_ln_g=conf_ln_g, conf_ln_b=conf_ln_b, conf_w_pw2=conf_w_pw2, conf_b_pw2=conf_b_pw2,
             hy_w_in=hy_w_in, hy_b_in=hy_b_in, hy_w_short=hy_w_short, hy_b_short=hy_b_short,
             hy_w_f0=hy_w_f0, hy_b_f0=hy_b_f0, hy_w_fin=hy_w_fin, hy_b_fin=hy_b_fin, hy_w_fout=hy_w_fout,
             hy_sin_freq=hy_sin_freq, hy_fft_bias=hy_fft_bias, hy_w_out=hy_w_out, hy_b_out=hy_b_out,
             moe_w_router=moe_w_router, moe_w_gate=moe_w_gate, moe_w_up=moe_w_up, moe_w_down=moe_w_down)
    for name in _BF16_WEIGHTS:
        p[name] = p[name].astype(BF16)
    return (_encoder(x_prompt, p), _encoder(x_sample, p))
```

```python
import functools
import math

import jax
import jax.numpy as jnp
from jax import lax
from jax.experimental import pallas as pl
from jax.experimental.pallas import tpu as pltpu

F32 = jnp.float32
BF16 = jnp.bfloat16
I32 = jnp.int32

RMS_EPS = 1e-6
LN_EPS = 1e-5
CONV_WIDTH = 31
SHORT_WIDTH = 3
EMB_DIM = 33
N_INNER_MLPS = 2
FAST_DECAY_PCT = 0.3
SLOW_DECAY_PCT = 1.5
DECAY_TARGET = 1e-2
N_EXPERTS = 16
CAPACITY_FACTOR = 2

LANE = 128
BF16_SUBLANE_TILE = 16
MXU_DIM = 256
VMEM_LIMIT = 56 * 1024 * 1024

ROW_TILE = 512
DFT_B = 128
GATHER_BLK = 256
GATHER_WIN = GATHER_BLK + BF16_SUBLANE_TILE
COMB_BLK = 512
COMB_WIN = 128
FFN_ROWS = 512


def _cparams(sem):
    return pltpu.CompilerParams(dimension_semantics=sem, vmem_limit_bytes=VMEM_LIMIT)


def _rms(x, g):
    return x * lax.rsqrt(jnp.mean(x * x, axis=-1, keepdims=True) + RMS_EPS) * g


def _norm_proj_kernel(x_ref, g_ref, w_ref, b_ref, o_ref):
    h = _rms(x_ref[...], g_ref[...]).astype(BF16)
    o_ref[...] = jnp.dot(h, w_ref[...], preferred_element_type=F32) + b_ref[...]


def _norm_glu_kernel(x_ref, g_ref, wa_ref, wg_ref, ba_ref, bg_ref, o_ref):
    h = _rms(x_ref[...], g_ref[...]).astype(BF16)
    a = jnp.dot(h, wa_ref[...], preferred_element_type=F32) + ba_ref[...]
    g = jnp.dot(h, wg_ref[...], preferred_element_type=F32) + bg_ref[...]
    o_ref[...] = a * jax.nn.sigmoid(g)


def _norm_proj(x, g, w, b, tn):
    t, d = x.shape
    n = w.shape[1]
    return pl.pallas_call(
        _norm_proj_kernel,
        grid=(t // ROW_TILE, n // tn),
        in_specs=[
            pl.BlockSpec((ROW_TILE, d), lambda i, j: (i, 0)),
            pl.BlockSpec((1, d), lambda i, j: (0, 0)),
            pl.BlockSpec((d, tn), lambda i, j: (0, j)),
            pl.BlockSpec((1, tn), lambda i, j: (0, j)),
        ],
        out_specs=pl.BlockSpec((ROW_TILE, tn), lambda i, j: (i, j)),
        out_shape=jax.ShapeDtypeStruct((t, n), F32),
        compiler_params=_cparams(("parallel", "parallel")),
        name="norm_proj",
    )(x, g.reshape(1, d), w, b.reshape(1, n))


def _norm_glu(x, g, w, b):
    t, d = x.shape
    n = w.shape[1] // 2
    return pl.pallas_call(
        _norm_glu_kernel,
        grid=(t // ROW_TILE,),
        in_specs=[
            pl.BlockSpec((ROW_TILE, d), lambda i: (i, 0)),
            pl.BlockSpec((1, d), lambda i: (0, 0)),
            pl.BlockSpec((d, n), lambda i: (0, 0)),
            pl.BlockSpec((d, n), lambda i: (0, 1)),
            pl.BlockSpec((1, n), lambda i: (0, 0)),
            pl.BlockSpec((1, n), lambda i: (0, 1)),
        ],
        out_specs=pl.BlockSpec((ROW_TILE, n), lambda i: (i, 0)),
        out_shape=jax.ShapeDtypeStruct((t, n), F32),
        compiler_params=_cparams(("parallel",)),
        name="norm_glu",
    )(x, g.reshape(1, d), w, w, b.reshape(1, 2 * n), b.reshape(1, 2 * n))


HALO = 16


CONV_CHUNK = 16


def _conf_tail_kernel(u_ref, up_ref, un_ref, x_ref, wdw_ref, bdw_ref, lng_ref, lnb_ref,
                      w2_ref, b2_ref, o_ref, ext_ref, acc_ref, *, tm, nblk):
    i = pl.program_id(1)
    d = u_ref.shape[-1]
    sl = d // LANE
    prev = up_ref[...].reshape(HALO, sl, LANE)
    nxt = un_ref[...].reshape(HALO, sl, LANE)
    ext_ref[pl.ds(0, HALO)] = jnp.where(i > 0, prev, jnp.zeros_like(prev))
    ext_ref[pl.ds(HALO, tm)] = u_ref[...].reshape(tm, sl, LANE)
    ext_ref[pl.ds(HALO + tm, HALO)] = jnp.where(i < nblk - 1, nxt, jnp.zeros_like(nxt))
    off = HALO - CONV_WIDTH // 2

    def chunk(c, carry):
        r0 = pl.multiple_of(c * CONV_CHUNK, CONV_CHUNK)
        acc = jnp.broadcast_to(bdw_ref[...], (CONV_CHUNK, sl, LANE))
        for k in range(CONV_WIDTH):
            acc = acc + ext_ref[pl.ds(r0 + off + k, CONV_CHUNK)] * wdw_ref[pl.ds(k, 1)]
        acc_ref[pl.ds(r0, CONV_CHUNK)] = acc
        return carry

    lax.fori_loop(0, tm // CONV_CHUNK, chunk, 0)
    acc = acc_ref[...].reshape(tm, d)
    mu = jnp.mean(acc, axis=-1, keepdims=True)
    xc = acc - mu
    var = jnp.mean(xc * xc, axis=-1, keepdims=True)
    y = xc * lax.rsqrt(var + LN_EPS) * lng_ref[...] + lnb_ref[...]
    y = (y * jax.nn.sigmoid(y)).astype(BF16)
    o_ref[...] = jnp.dot(y, w2_ref[...], preferred_element_type=F32) + b2_ref[...] + x_ref[...]


def _conf_tail(u, x, w_dw, b_dw, ln_g, ln_b, w2, b2):
    bt, l, d = u.shape
    tm = ROW_TILE
    nblk = l // tm
    hb = tm // HALO
    sl = d // LANE
    wpad = jnp.zeros((32, d), F32).at[:CONV_WIDTH].set(w_dw).reshape(32, sl, LANE)
    row = lambda a: a.reshape(1, d)
    return pl.pallas_call(
        functools.partial(_conf_tail_kernel, tm=tm, nblk=nblk),
        grid=(bt, nblk),
        in_specs=[
            pl.BlockSpec((None, tm, d), lambda b, i: (b, i, 0)),
            pl.BlockSpec((None, HALO, d), lambda b, i: (b, jnp.maximum(i * hb - 1, 0), 0)),
            pl.BlockSpec((None, HALO, d), lambda b, i: (b, jnp.minimum((i + 1) * hb, nblk * hb - 1), 0)),
            pl.BlockSpec((None, tm, d), lambda b, i: (b, i, 0)),
            pl.BlockSpec((32, sl, LANE), lambda b, i: (0, 0, 0)),
            pl.BlockSpec((1, sl, LANE), lambda b, i: (0, 0, 0)),
            pl.BlockSpec((1, d), lambda b, i: (0, 0)),
            pl.BlockSpec((1, d), lambda b, i: (0, 0)),
            pl.BlockSpec((d, d), lambda b, i: (0, 0)),
            pl.BlockSpec((1, d), lambda b, i: (0, 0)),
        ],
        out_specs=pl.BlockSpec((None, tm, d), lambda b, i: (b, i, 0)),
        out_shape=jax.ShapeDtypeStruct((bt, l, d), F32),
        scratch_shapes=[pltpu.VMEM((tm + 2 * HALO, sl, LANE), F32), pltpu.VMEM((tm, sl, LANE), F32)],
        compiler_params=_cparams(("parallel", "parallel")),
        name="conf_tail",
    )(u, u, u, x, wpad, b_dw.reshape(1, sl, LANE), row(ln_g), row(ln_b), w2, row(b2))


SHALO = 8


def _short_conv_kernel(u_ref, up_ref, un_ref, w_ref, b_ref, vg_ref, vgm_ref, x0_ref, ext_ref, *, tm, nblk, d):
    i = pl.program_id(1)
    prev = up_ref[...]
    nxt = un_ref[...]
    ext_ref[pl.ds(0, SHALO), :] = jnp.where(i > 0, prev, jnp.zeros_like(prev))
    ext_ref[pl.ds(SHALO, tm), :] = u_ref[...]
    ext_ref[pl.ds(SHALO + tm, SHALO), :] = jnp.where(i < nblk - 1, nxt, jnp.zeros_like(nxt))

    def conv(c):
        cols = pl.ds(c * d, d)
        acc = b_ref[:, cols]
        for k in range(SHORT_WIDTH):
            acc = acc + ext_ref[pl.ds(SHALO - 1 + k, tm), cols] * w_ref[pl.ds(k, 1), cols]
        return acc

    x0_ref[...] = conv(0)
    vg = conv(2) * conv(1)
    vg_ref[...] = vg
    _to_group_major(vg, vgm_ref, tm)


DFT_G = DFT_B // 8


def _to_group_major(val, ref, tm):
    for ai in range(tm // DFT_B):
        for g in range(DFT_G):
            ref[g, pl.ds(ai * 8, 8), :] = val[ai * DFT_B + g * 8: ai * DFT_B + g * 8 + 8, :]


def _from_group_major(ref, tm):
    return jnp.concatenate([ref[g, pl.ds(ai * 8, 8), :] for ai in range(tm // DFT_B) for g in range(DFT_G)],
                           axis=0)


def _short_conv(u3, w_short, b_short):
    bt, l, d3 = u3.shape
    d = d3 // 3
    tm = ROW_TILE
    nblk = l // tm
    hb = tm // SHALO
    gm_rows = 8 * tm // DFT_B
    wpad = jnp.zeros((8, d3), F32).at[:SHORT_WIDTH].set(w_short)
    out = jax.ShapeDtypeStruct((bt, l, d), F32)
    return pl.pallas_call(
        functools.partial(_short_conv_kernel, tm=tm, nblk=nblk, d=d),
        grid=(bt, nblk),
        in_specs=[
            pl.BlockSpec((None, tm, d3), lambda b, i: (b, i, 0)),
            pl.BlockSpec((None, SHALO, d3), lambda b, i: (b, jnp.maximum(i * hb - 1, 0), 0)),
            pl.BlockSpec((None, SHALO, d3), lambda b, i: (b, jnp.minimum((i + 1) * hb, nblk * hb - 1), 0)),
            pl.BlockSpec((8, d3), lambda b, i: (0, 0)),
            pl.BlockSpec((1, d3), lambda b, i: (0, 0)),
        ],
        out_specs=[
            pl.BlockSpec((None, tm, d), lambda b, i: (b, i, 0)),
            pl.BlockSpec((None, DFT_G, gm_rows, d), lambda b, i: (b, 0, i, 0)),
            pl.BlockSpec((None, tm, d), lambda b, i: (b, i, 0)),
        ],
        out_shape=[out, jax.ShapeDtypeStruct((bt, DFT_G, 8 * l // DFT_B, d), F32), out],
        scratch_shapes=[pltpu.VMEM((tm + 2 * SHALO, d3), F32)],
        compiler_params=_cparams(("parallel", "parallel")),
        name="short_conv",
    )(u3, u3, u3, wpad, b_short.reshape(1, d3))


_HI16 = -65536


def _pack_complex(re, im):
    hi = lax.bitcast_convert_type(re.astype(BF16).astype(F32), I32)
    lo = lax.bitcast_convert_type(im.astype(BF16).astype(F32), I32)
    return hi | lax.shift_right_logical(lo, jnp.int32(16))


def _unpack_complex(w):
    re = lax.bitcast_convert_type(w & jnp.int32(_HI16), F32)
    im = lax.bitcast_convert_type(lax.shift_left(w, jnp.int32(16)), F32)
    return re.astype(BF16), im.astype(BF16)


def _dft1_kernel(v_ref, fa_ref, x_ref, *, a):
    r = jnp.dot(fa_ref[...], v_ref[...].astype(BF16), preferred_element_type=F32)
    x_ref[...] = _pack_complex(r[:8 * a], r[8 * a:])


def _dft2_kernel(x_ref, mf_ref, mi_ref, kr_ref, ki_ref, z_ref, *, at, b):
    d = x_ref.shape[-1]
    for j in range(at):
        xr, xi = _unpack_complex(x_ref[:, j].reshape(b, d))
        v = jnp.dot(mf_ref[j], jnp.concatenate([xr, xi], axis=0), preferred_element_type=F32)
        vr, vi = v[:b], v[b:]
        kr, ki = kr_ref[j], ki_ref[j]
        yr = (vr * kr - vi * ki).astype(BF16)
        yi = (vr * ki + vi * kr).astype(BF16)
        z = jnp.dot(mi_ref[j], jnp.concatenate([yr, yi], axis=0), preferred_element_type=F32)
        z_ref[:, j] = _pack_complex(z[:b], z[b:]).reshape(DFT_G, 8, d)


def _dft3_kernel(z_ref, fi_ref, y_ref):
    zr, zi = _unpack_complex(z_ref[...])
    y_ref[...] = jnp.dot(fi_ref[...], jnp.concatenate([zr, zi], axis=0), preferred_element_type=F32)


def _dft_constants(l):
    n = 2 * l
    b = DFT_B
    a = n // b
    ah = a // 2
    two_pi = 2.0 * math.pi
    f1 = jnp.arange(a, dtype=F32)[:, None]
    aa = jnp.arange(ah, dtype=F32)[None, :]
    ang_a = two_pi * ((f1 * aa) % a) / a
    eye8 = jnp.eye(8, dtype=F32)
    fa = jnp.kron(jnp.concatenate([jnp.cos(ang_a), -jnp.sin(ang_a)], axis=0), eye8).astype(BF16)
    fi = jnp.kron(jnp.concatenate([jnp.cos(ang_a).T, -jnp.sin(ang_a).T], axis=1), eye8).astype(BF16)
    bb = jnp.arange(b, dtype=F32)
    ang_b = two_pi * ((bb[:, None] * bb[None, :]) % b) / b
    cb, sb = jnp.cos(ang_b)[None], jnp.sin(ang_b)[None]
    ang_t = two_pi * (jnp.arange(a, dtype=F32)[:, None] * bb[None, :]) / n
    ct, st = jnp.cos(ang_t), jnp.sin(ang_t)
    block = lambda re, im: jnp.concatenate([jnp.concatenate([re, -im], 2), jnp.concatenate([im, re], 2)], 1)
    tc, ts = ct[:, None, :], st[:, None, :]
    mf = block(cb * tc - sb * ts, -(cb * ts + sb * tc)).astype(BF16)
    rc, rs = ct[:, :, None], st[:, :, None]
    mi = block(rc * cb - rs * sb, rc * sb + rs * cb).astype(BF16)
    return dict(a=a, ah=ah, fa=fa, fi=fi, mf=mf, mi=mi)


def _dft_stage1(vgm, c):
    bt, _, _, d = vgm.shape
    a, ah = c["a"], c["ah"]
    return pl.pallas_call(
        functools.partial(_dft1_kernel, a=a),
        grid=(bt, DFT_G),
        in_specs=[
            pl.BlockSpec((None, None, 8 * ah, d), lambda n_, g: (n_, g, 0, 0)),
            pl.BlockSpec((16 * a, 8 * ah), lambda n_, g: (0, 0)),
        ],
        out_specs=pl.BlockSpec((None, None, 8 * a, d), lambda n_, g: (n_, g, 0, 0)),
        out_shape=jax.ShapeDtypeStruct((bt, DFT_G, 8 * a, d), I32),
        compiler_params=_cparams(("parallel", "parallel")),
        name="dft_stage1",
    )(vgm, c["fa"])


SPEC_F1 = 4


def _filter_spec_kernel(x_ref, mf_ref, s_ref, kr_ref, ki_ref, *, b):
    d = x_ref.shape[-1]
    for j in range(SPEC_F1):
        spec = []
        for side in range(2):
            xr, xi = _unpack_complex(x_ref[side, :, j].reshape(b, d))
            spec.append(jnp.dot(mf_ref[j], jnp.concatenate([xr, xi], axis=0), preferred_element_type=F32))
        fwd, bwd = spec
        kr_ref[j] = (fwd[:b] + bwd[:b]) * s_ref[...]
        ki_ref[j] = (fwd[b:] - bwd[b:]) * s_ref[...]


def _filter_spectrum(hgm, scale, c):
    d = hgm.shape[-1]
    a, b = c["a"], DFT_B
    x = _dft_stage1(hgm, c)
    out = jax.ShapeDtypeStruct((a, b, d), F32)
    return pl.pallas_call(
        functools.partial(_filter_spec_kernel, b=b),
        grid=(a // SPEC_F1,),
        in_specs=[
            pl.BlockSpec((2, DFT_G, SPEC_F1, 8, d), lambda i: (0, 0, i, 0, 0)),
            pl.BlockSpec((SPEC_F1, 2 * b, 2 * b), lambda i: (i, 0, 0)),
            pl.BlockSpec((1, d), lambda i: (0, 0)),
        ],
        out_specs=[pl.BlockSpec((SPEC_F1, b, d), lambda i: (i, 0, 0))] * 2,
        out_shape=[out, out],
        compiler_params=_cparams(("parallel",)),
        name="filter_spectrum",
    )(x.reshape(2, DFT_G, a, 8, d), c["mf"], scale.reshape(1, d))


def _long_conv(vgm, kr, ki, c):
    bt, _, _, d = vgm.shape
    a, ah, b = c["a"], c["ah"], DFT_B
    mf, mi, fi = c["mf"], c["mi"], c["fi"]
    x = _dft_stage1(vgm, c)

    at = SPEC_F1
    z = pl.pallas_call(
        functools.partial(_dft2_kernel, at=at, b=b),
        grid=(a // at, bt),
        in_specs=[
            pl.BlockSpec((None, DFT_G, at, 8, d), lambda i, n_: (n_, 0, i, 0, 0)),
            pl.BlockSpec((at, 2 * b, 2 * b), lambda i, n_: (i, 0, 0)),
            pl.BlockSpec((at, 2 * b, 2 * b), lambda i, n_: (i, 0, 0)),
            pl.BlockSpec((at, b, d), lambda i, n_: (i, 0, 0)),
            pl.BlockSpec((at, b, d), lambda i, n_: (i, 0, 0)),
        ],
        out_specs=pl.BlockSpec((None, DFT_G, at, 8, d), lambda i, n_: (n_, 0, i, 0, 0)),
        out_shape=jax.ShapeDtypeStruct((bt, DFT_G, a, 8, d), I32),
        compiler_params=_cparams(("parallel", "parallel")),
        name="dft_stage2",
    )(x.reshape(bt, DFT_G, a, 8, d), mf, mi, kr, ki)

    return pl.pallas_call(
        _dft3_kernel,
        grid=(bt, DFT_G),
        in_specs=[
            pl.BlockSpec((None, None, 8 * a, d), lambda n_, g: (n_, g, 0, 0)),
            pl.BlockSpec((8 * ah, 16 * a), lambda n_, g: (0, 0)),
        ],
        out_specs=pl.BlockSpec((None, None, 8 * ah, d), lambda n_, g: (n_, g, 0, 0)),
        out_shape=jax.ShapeDtypeStruct((bt, DFT_G, 8 * ah, d), F32),
        compiler_params=_cparams(("parallel", "parallel")),
        name="dft_stage3",
    )(z.reshape(bt, DFT_G, 8 * a, d), fi)


def _filter_taps_kernel(fv_ref, w0_ref, b0_ref, win_ref, bin_ref, wout_ref, freq_ref, dl_ref,
                        h_ref, s_ref, *, tm, l, d):
    hp = lax.Precision.HIGHEST
    n = (pl.program_id(0) * tm + lax.broadcasted_iota(I32, (tm, 1), 0)).astype(F32)
    t = n / (l - 1)
    arg = (2.0 * math.pi / l) * n * fv_ref[...]
    lane = lax.broadcasted_iota(I32, (tm, LANE), 1)
    bands = (EMB_DIM - 1) // 2
    z = jnp.where(lane == 0, t, jnp.where(lane <= bands, jnp.cos(arg),
                                          jnp.where(lane <= 2 * bands, -jnp.sin(arg), 0.0)))
    freq = freq_ref[...]
    hf = jnp.sin(freq * (jnp.dot(z, w0_ref[...], precision=hp, preferred_element_type=F32) + b0_ref[...]))
    for i in range(N_INNER_MLPS):
        hf = jnp.sin(freq * (jnp.dot(hf, win_ref[i], precision=hp, preferred_element_type=F32) + bin_ref[i]))
    hf = jnp.dot(hf, wout_ref[...], precision=hp, preferred_element_type=F32)
    decay = jnp.exp(-t * jnp.abs(dl_ref[...]))
    h_fwd = hf[:, :d] * decay
    h_bwd0 = jnp.where(n > 0.0, hf[:, d:] * decay, 0.0)
    _to_group_major(h_fwd, h_ref.at[0], tm)
    _to_group_major(h_bwd0, h_ref.at[1], tm)
    part = jnp.sum(jnp.abs(h_fwd) + jnp.abs(h_bwd0), axis=0, keepdims=True)
    s_ref[...] = jnp.broadcast_to(part, s_ref.shape)


def _hyena_filter_taps(l, w_f0, b_f0, w_fin, b_fin, w_fout, sin_freq):
    d = w_fout.shape[-1] // 2
    o = w_f0.shape[1]
    tm = ROW_TILE
    nblk = l // tm
    bands = (EMB_DIM - 1) // 2
    f = jnp.linspace(1e-4, bands - 1, bands, dtype=F32)
    fvec = jnp.zeros((1, LANE), F32).at[0, 1:1 + bands].set(f).at[0, 1 + bands:1 + 2 * bands].set(f)
    pad = lambda a, shape: jnp.zeros(shape, F32).at[tuple(slice(0, s) for s in a.shape)].set(a)
    max_decay = math.log(DECAY_TARGET) / FAST_DECAY_PCT
    min_decay = math.log(DECAY_TARGET) / SLOW_DECAY_PCT
    deltas = jnp.linspace(min_decay, max_decay, d, dtype=F32)[None, :]
    full = lambda *shape: pl.BlockSpec(shape, lambda i: (0,) * len(shape))
    hgm, sums = pl.pallas_call(
        functools.partial(_filter_taps_kernel, tm=tm, l=l, d=d),
        grid=(nblk,),
        in_specs=[full(1, LANE), full(LANE, LANE), full(1, LANE), full(N_INNER_MLPS, LANE, LANE),
                  full(N_INNER_MLPS, 1, LANE), full(LANE, 2 * d), full(1, LANE), full(1, d)],
        out_specs=[pl.BlockSpec((2, DFT_G, 8 * tm // DFT_B, d), lambda i: (0, 0, i, 0)),
                   pl.BlockSpec((None, 8, d), lambda i: (i, 0, 0))],
        out_shape=[jax.ShapeDtypeStruct((2, DFT_G, 8 * l // DFT_B, d), F32),
                   jax.ShapeDtypeStruct((nblk, 8, d), F32)],
        compiler_params=_cparams(("parallel",)),
        name="filter_taps",
    )(fvec, pad(w_f0, (LANE, LANE)), pad(b_f0[None], (1, LANE)), pad(w_fin, (N_INNER_MLPS, LANE, LANE)),
      pad(b_fin[:, None], (N_INNER_MLPS, 1, LANE)), pad(w_fout, (LANE, 2 * d)), pad(sin_freq[None], (1, LANE)), deltas)
    assert o <= LANE and EMB_DIM <= LANE
    return hgm, 1.0 / (jnp.sum(sums[:, 0, :], axis=0) * (2 * l))


def _hyena_tail_kernel(y_ref, v_ref, x0_ref, fb_ref, w_ref, b_ref, x_ref, o_ref, *, tm):
    y = _from_group_major(y_ref, tm)
    g = ((y + v_ref[...] * fb_ref[...]) * x0_ref[...]).astype(BF16)
    o_ref[...] = jnp.dot(g, w_ref[...], preferred_element_type=F32) + b_ref[...] + x_ref[...]


def _hyena_tail(ygm, v, x0, fft_bias, w_out, b_out, x):
    bt, l, d = x.shape
    tm = ROW_TILE
    tile = pl.BlockSpec((None, tm, d), lambda b, i: (b, i, 0))
    rowv = pl.BlockSpec((1, d), lambda b, i: (0, 0))
    return pl.pallas_call(
        functools.partial(_hyena_tail_kernel, tm=tm),
        grid=(bt, l // tm),
        in_specs=[pl.BlockSpec((None, DFT_G, 8 * tm // DFT_B, d), lambda b, i: (b, 0, i, 0)),
                  tile, tile, rowv, pl.BlockSpec((d, d), lambda b, i: (0, 0)), rowv, tile],
        out_specs=tile,
        out_shape=jax.ShapeDtypeStruct((bt, l, d), F32),
        compiler_params=_cparams(("parallel", "parallel")),
        name="hyena_tail",
    )(ygm, v, x0, fft_bias.reshape(1, d), w_out, b_out.reshape(1, d), x)


def _router_kernel(x_ref, g_ref, wt_ref, hn_ref, pt_ref):
    h = _rms(x_ref[...], g_ref[...])
    hn_ref[...] = h.astype(BF16)
    lt = lax.dot_general(wt_ref[...], h, (((1,), (1,)), ((), ())), precision=lax.Precision.HIGHEST,
                         preferred_element_type=F32)
    et = jnp.exp(lt - jnp.max(lt, axis=0, keepdims=True))
    pt_ref[...] = et / jnp.sum(et, axis=0, keepdims=True)


def _router(x, g, w_router):
    t, d = x.shape
    e = w_router.shape[1]
    return pl.pallas_call(
        _router_kernel,
        grid=(t // ROW_TILE,),
        in_specs=[
            pl.BlockSpec((ROW_TILE, d), lambda i: (i, 0)),
            pl.BlockSpec((1, d), lambda i: (0, 0)),
            pl.BlockSpec((e, d), lambda i: (0, 0)),
        ],
        out_specs=[
            pl.BlockSpec((ROW_TILE, d), lambda i: (i, 0)),
            pl.BlockSpec((e, ROW_TILE), lambda i: (0, i)),
        ],
        out_shape=[
            jax.ShapeDtypeStruct((t, d), BF16),
            jax.ShapeDtypeStruct((e, t), F32),
        ],
        compiler_params=_cparams(("parallel",)),
        name="router",
    )(x, g.reshape(1, d), w_router.T)


CUM_BLK = 256


def _select_kernel(p_ref, posm_ref, posx_ref, mask_ref, *, cap, t):
    keys = pltpu.bitcast(p_ref[...], I32)
    e = keys.shape[0]

    def vbit(i, tau):
        cand = tau | jnp.left_shift(jnp.int32(1), 30 - i)
        cnt = jnp.sum((keys >= cand).astype(I32), axis=1, keepdims=True)
        return jnp.where(cnt >= cap, cand, tau)

    tau = lax.fori_loop(0, 31, vbit, jnp.zeros((e, 1), I32))
    gt = keys > tau
    eq = keys == tau
    quota = cap - jnp.sum(gt.astype(I32), axis=1, keepdims=True)
    idx = lax.broadcasted_iota(I32, keys.shape, 1)
    nbits = max(1, (t - 1).bit_length())

    def ibit(i, jj):
        cand = jj | jnp.left_shift(jnp.int32(1), nbits - 1 - i)
        cnt = jnp.sum((eq & (idx < cand)).astype(I32), axis=1, keepdims=True)
        return jnp.where(cnt < quota, cand, jj)

    jlast = lax.fori_loop(0, nbits, ibit, jnp.zeros((e, 1), I32))
    mask = gt | (eq & (idx <= jlast))
    mask_ref[...] = mask.astype(F32)

    r = lax.broadcasted_iota(I32, (CUM_BLK, CUM_BLK), 0)
    c = lax.broadcasted_iota(I32, (CUM_BLK, CUM_BLK), 1)
    tri = (r <= c).astype(BF16)
    run = jnp.zeros((e, 1), F32)
    for blk in range(t // CUM_BLK):
        cols = pl.ds(blk * CUM_BLK, CUM_BLK)
        m = mask_ref[:, cols]
        incl = jnp.dot(m.astype(BF16), tri, preferred_element_type=F32)
        pos = (incl - m + run).astype(I32)
        posx_ref[:, cols] = pos
        posm_ref[:, cols] = jnp.where(m > 0.0, pos, -1)
        run = run + incl[:, CUM_BLK - 1:CUM_BLK]


def _select(probs_t, cap):
    e, t = probs_t.shape
    full = pl.BlockSpec((e, t), lambda: (0, 0))
    return pl.pallas_call(
        functools.partial(_select_kernel, cap=cap, t=t),
        in_specs=[full],
        out_specs=[full, full],
        out_shape=[jax.ShapeDtypeStruct((e, t), I32), jax.ShapeDtypeStruct((e, t), I32)],
        scratch_shapes=[pltpu.VMEM((e, t), F32)],
        compiler_params=pltpu.CompilerParams(vmem_limit_bytes=VMEM_LIMIT),
        name="select",
    )(probs_t)


GATHER_EXPERTS = 2


GATHER_ROWS = 64


def _gather_kernel(st_ref, nch_ref, hn_ref, posm_ref, p_ref, o_ref, g_ref, *, sub):
    eg = pl.program_id(0)
    k = pl.program_id(1)

    @pl.when(k == 0)
    def _():
        o_ref[...] = jnp.zeros_like(o_ref)
        g_ref[...] = jnp.zeros_like(g_ref)

    rows = lax.broadcasted_iota(I32, (GATHER_ROWS, GATHER_BLK), 0)

    def onehot(j, x, chunk):
        tok = pl.ds(j * GATHER_BLK, GATHER_BLK)
        base = (pl.multiple_of(st_ref[eg * GATHER_EXPERTS + x, k * sub + j], BF16_SUBLANE_TILE)
                + chunk * GATHER_ROWS)
        hit = rows == posm_ref[x, :, tok] - base
        gate = jnp.sum(jnp.where(hit, p_ref[x, :, tok], 0.0), axis=1, keepdims=True)
        return base, hit.astype(BF16), gate

    def add_rows(x, base, slab, gate):
        win = pl.ds(pl.multiple_of(base, BF16_SUBLANE_TILE), GATHER_ROWS)
        o_ref[x, win, :] = o_ref[x, win, :] + slab.astype(BF16)
        g_ref[x, win, :] = g_ref[x, win, :] + gate

    for j in range(sub):
        first = [onehot(j, x, 0) for x in range(GATHER_EXPERTS)]
        slab = jnp.dot(jnp.concatenate([oh for _, oh, _ in first], axis=0),
                       hn_ref[pl.ds(j * GATHER_BLK, GATHER_BLK), :], preferred_element_type=F32)
        for x in range(GATHER_EXPERTS):
            add_rows(x, first[x][0], slab[x * GATHER_ROWS:(x + 1) * GATHER_ROWS], first[x][2])

    for j in range(sub):
        for x in range(GATHER_EXPERTS):
            def extra(chunk, carry):
                base, oh, gate = onehot(j, x, chunk)
                add_rows(x, base, jnp.dot(oh, hn_ref[pl.ds(j * GATHER_BLK, GATHER_BLK), :],
                                          preferred_element_type=F32), gate)
                return carry

            lax.fori_loop(1, nch_ref[eg * GATHER_EXPERTS + x, k * sub + j], extra, 0)


def _gather_tokens(hn, posm, probs_t, starts, nchunks, cap):
    t, d = hn.shape
    e = posm.shape[0]
    sub = 4
    nb = t // (GATHER_BLK * sub)
    cp = cap + GATHER_WIN
    return pl.pallas_call(
        functools.partial(_gather_kernel, sub=sub),
        grid_spec=pltpu.PrefetchScalarGridSpec(
            num_scalar_prefetch=2,
            grid=(e // GATHER_EXPERTS, nb),
            in_specs=[
                pl.BlockSpec((GATHER_BLK * sub, d), lambda ei, k, st, nc: (k, 0)),
                pl.BlockSpec((GATHER_EXPERTS, 1, GATHER_BLK * sub), lambda ei, k, st, nc: (ei, 0, k)),
                pl.BlockSpec((GATHER_EXPERTS, 1, GATHER_BLK * sub), lambda ei, k, st, nc: (ei, 0, k)),
            ],
            out_specs=[pl.BlockSpec((GATHER_EXPERTS, cp, d), lambda ei, k, st, nc: (ei, 0, 0)),
                       pl.BlockSpec((GATHER_EXPERTS, cp, LANE), lambda ei, k, st, nc: (ei, 0, 0))],
        ),
        out_shape=[jax.ShapeDtypeStruct((e, cp, d), BF16), jax.ShapeDtypeStruct((e, cp, LANE), F32)],
        compiler_params=_cparams(("arbitrary", "arbitrary")),
        name="gather_tokens",
    )(starts, nchunks, hn, posm.reshape(e, 1, t), probs_t.reshape(e, 1, t))


def _ffn_kernel(x_ref, gate_ref, wg_ref, wu_ref, wd_ref, o_ref, *, nrt, splits):
    r = pl.program_id(1)

    @pl.when(r < nrt)
    def _():
        x = x_ref[...]
        dn = None
        for c0, c1 in splits:
            g = jnp.dot(x, wg_ref[:, c0:c1], preferred_element_type=F32)
            u = jnp.dot(x, wu_ref[:, c0:c1], preferred_element_type=F32)
            h = (g * jax.nn.sigmoid(g) * u).astype(BF16)
            part = jnp.dot(h, wd_ref[c0:c1, :], preferred_element_type=F32)
            dn = part if dn is None else dn + part
        o_ref[...] = (dn * gate_ref[:, 0:1]).astype(BF16)

    @pl.when(r == nrt)
    def _():
        o_ref[...] = jnp.zeros_like(o_ref)


def _expert_ffn(xs, gates, w_gate, w_up, w_down, layer, cap):
    e, _, d = xs.shape
    f = w_gate.shape[-1]
    nrt = cap // FFN_ROWS
    half = (f // MXU_DIM + 1) // 2 * MXU_DIM
    splits = ((0, half), (half, f))
    row = lambda r: jnp.minimum(r, nrt - 1)
    return pl.pallas_call(
        functools.partial(_ffn_kernel, nrt=nrt, splits=splits),
        grid=(e, nrt + 1),
        in_specs=[
            pl.BlockSpec((None, FFN_ROWS, d), lambda ei, r: (ei, row(r), 0)),
            pl.BlockSpec((None, FFN_ROWS, LANE), lambda ei, r: (ei, row(r), 0)),
            pl.BlockSpec((None, None, d, f), lambda ei, r: (layer, ei, 0, 0)),
            pl.BlockSpec((None, None, d, f), lambda ei, r: (layer, ei, 0, 0)),
            pl.BlockSpec((None, None, f, d), lambda ei, r: (layer, ei, 0, 0)),
        ],
        out_specs=pl.BlockSpec((None, FFN_ROWS, d), lambda ei, r: (ei, r, 0)),
        out_shape=jax.ShapeDtypeStruct((e, cap + FFN_ROWS, d), BF16),
        compiler_params=_cparams(("parallel", "arbitrary")),
        name="expert_ffn",
    )(xs, gates, w_gate, w_up, w_down)


def _combine_kernel(st_ref, nch_ref, x_ref, posm_ref, g_ref, out_hbm, o_ref,
                    stack, lhs, xbuf, sem, xsem, *, ne, cp, final_norm):
    b = pl.program_id(0)
    nsteps = pl.num_programs(0)

    def window_copies(step, slot):
        for ei in range(ne):
            start = pl.multiple_of(ei * cp + st_ref[ei, step], BF16_SUBLANE_TILE)
            dst = stack.at[slot, pl.ds(ei * COMB_WIN, COMB_WIN), :]
            yield pltpu.make_async_copy(out_hbm.at[pl.ds(start, COMB_WIN), :], dst, sem.at[slot])

    @pl.when(b == 0)
    def _():
        for c in window_copies(0, 0):
            c.start()

    @pl.when(b + 1 < nsteps)
    def _():
        for c in window_copies(b + 1, (b + 1) % 2):
            c.start()

    cols = lax.broadcasted_iota(I32, (COMB_BLK, COMB_WIN), 1)

    def onehot(ei, chunk):
        rel = posm_ref[:, ei:ei + 1] - st_ref[ei, b] - chunk * COMB_WIN
        return (cols == rel).astype(BF16)

    for ei in range(ne):
        lhs[:, pl.ds(ei * COMB_WIN, COMB_WIN)] = onehot(ei, 0)

    slot = b % 2
    for c in window_copies(b, slot):
        c.wait()
    o_ref[...] = x_ref[...] + jnp.dot(lhs[...], stack[slot], preferred_element_type=F32)

    for ei in range(ne):
        def extra(chunk, carry):
            start = pl.multiple_of(ei * cp + st_ref[ei, b] + chunk * COMB_WIN, BF16_SUBLANE_TILE)
            cpy = pltpu.make_async_copy(out_hbm.at[pl.ds(start, COMB_WIN), :], xbuf, xsem)
            cpy.start()
            cpy.wait()
            o_ref[...] = o_ref[...] + jnp.dot(onehot(ei, chunk), xbuf[...], preferred_element_type=F32)
            return carry

        lax.fori_loop(1, nch_ref[ei, b], extra, 0)

    if final_norm:
        o_ref[...] = _rms(o_ref[...], g_ref[...])


def _combine(x, posm_tok, out_rows, starts, nchunks, final_g):
    t, d = x.shape
    e = posm_tok.shape[1]
    cp = out_rows.shape[0] // e
    final_norm = final_g is not None
    g = final_g if final_norm else jnp.ones((d,), F32)
    return pl.pallas_call(
        functools.partial(_combine_kernel, ne=e, cp=cp, final_norm=final_norm),
        grid_spec=pltpu.PrefetchScalarGridSpec(
            num_scalar_prefetch=2,
            grid=(t // COMB_BLK,),
            in_specs=[
                pl.BlockSpec((COMB_BLK, d), lambda i, st, nc: (i, 0)),
                pl.BlockSpec((COMB_BLK, e), lambda i, st, nc: (i, 0)),
                pl.BlockSpec((1, d), lambda i, st, nc: (0, 0)),
                pl.BlockSpec(memory_space=pl.ANY),
            ],
            out_specs=pl.BlockSpec((COMB_BLK, d), lambda i, st, nc: (i, 0)),
            scratch_shapes=[
                pltpu.VMEM((2, e * COMB_WIN, d), BF16),
                pltpu.VMEM((COMB_BLK, e * COMB_WIN), BF16),
                pltpu.VMEM((COMB_WIN, d), BF16),
                pltpu.SemaphoreType.DMA((2,)),
                pltpu.SemaphoreType.DMA,
            ],
        ),
        out_shape=jax.ShapeDtypeStruct((t, d), F32),
        compiler_params=_cparams(("arbitrary",)),
        name="combine",
    )(starts, nchunks, x, posm_tok, g.reshape(1, d), out_rows)


def _window_tables(posx, blk, win, cap):
    e = posx.shape[0]
    cum = posx[:, ::blk]
    cnt = jnp.concatenate([cum[:, 1:], jnp.full((e, 1), cap, I32)], axis=1) - cum
    starts = (cum // BF16_SUBLANE_TILE) * BF16_SUBLANE_TILE
    return starts, (cum - starts + cnt + win - 1) // win


def _ec_moe(x, g, w_router, w_gate, w_up, w_down, layer, final_g):
    t, d = x.shape
    e = w_router.shape[1]
    cap = max(1, min(t, CAPACITY_FACTOR * t // e))
    hn, probs_t = _router(x, g, w_router)
    posm, posx = _select(probs_t, cap)
    xs, gates = _gather_tokens(hn, posm, probs_t, *_window_tables(posx, GATHER_BLK, GATHER_ROWS, cap), cap)
    out_rows = _expert_ffn(xs, gates, w_gate, w_up, w_down, layer, cap)
    return _combine(x, posm.T, out_rows.reshape(-1, d), *_window_tables(posx, COMB_BLK, COMB_WIN, cap), final_g)


def _encoder(x3, p):
    bt, l, d = x3.shape
    t = bt * l
    x = x3.reshape(t, d)
    depth = p["norm_mix_g"].shape[0]
    for i in range(depth):
        j = i // 2
        if i % 2 == 0:
            u = _norm_glu(x, p["norm_mix_g"][i], p["conf_w_pw1"][j], p["conf_b_pw1"][j])
            x = _conf_tail(u.reshape(bt, l, d), x.reshape(bt, l, d), p["conf_w_dw"][j], p["conf_b_dw"][j],
                           p["conf_ln_g"][j], p["conf_ln_b"][j], p["conf_w_pw2"][j],
                           p["conf_b_pw2"][j]).reshape(t, d)
        else:
            u3 = _norm_proj(x, p["norm_mix_g"][i], p["hy_w_in"][j], p["hy_b_in"][j], tn=3 * d)
            vg, vgm, x0 = _short_conv(u3.reshape(bt, l, 3 * d), p["hy_w_short"][j], p["hy_b_short"][j])
            c = _dft_constants(l)
            taps = _hyena_filter_taps(l, p["hy_w_f0"][j], p["hy_b_f0"][j], p["hy_w_fin"][j],
                                      p["hy_b_fin"][j], p["hy_w_fout"][j], p["hy_sin_freq"][j])
            kr, ki = _filter_spectrum(*taps, c)
            ygm = _long_conv(vgm, kr, ki, c)
            x = _hyena_tail(ygm, vg, x0, p["hy_fft_bias"][j], p["hy_w_out"][j], p["hy_b_out"][j],
                            x.reshape(bt, l, d)).reshape(t, d)
        final_g = p["final_norm_g"] if i == depth - 1 else None
        x = _ec_moe(x, p["norm_ffn_g"][i], p["moe_w_router"][i], p["moe_w_gate"], p["moe_w_up"],
                    p["moe_w_down"], i, final_g)
    return x.reshape(bt, l, d)


_BF16_WEIGHTS = ("conf_w_pw1", "conf_w_pw2", "hy_w_in", "hy_w_out", "moe_w_gate", "moe_w_up", "moe_w_down")


def kernel(x_prompt, x_sample, norm_mix_g, norm_ffn_g, final_norm_g, conf_w_pw1, conf_b_pw1, conf_w_dw, conf_b_dw, conf_ln_g, conf_ln_b, conf_w_pw2, conf_b_pw2, hy_w_in, hy_b_in, hy_w_short, hy_b_short, hy_w_f0, hy_b_f0, hy_w_fin, hy_b_fin, hy_w_fout, hy_sin_freq, hy_fft_bias, hy_w_out, hy_b_out, moe_w_router, moe_w_gate, moe_w_up, moe_w_down):
    p = dict(norm_mix_g=norm_mix_g, norm_ffn_g=norm_ffn_g, final_norm_g=final_norm_g,
             conf_w_pw1=conf_w_pw1, conf_b_pw1=conf_b_pw1, conf_w_dw=conf_w_dw, conf_b_dw=conf_b_dw,
             conf_ln_g=conf_ln_g, conf_ln_b=conf_ln_b, conf_w_pw2=conf_w_pw2, conf_b_pw2=conf_b_pw2,
             hy_w_in=hy_w_in, hy_b_in=hy_b_in, hy_w_short=hy_w_short, hy_b_short=hy_b_short,
             hy_w_f0=hy_w_f0, hy_b_f0=hy_b_f0, hy_w_fin=hy_w_fin, hy_b_fin=hy_b_fin, hy_w_fout=hy_w_fout,
             hy_sin_freq=hy_sin_freq, hy_fft_bias=hy_fft_bias, hy_w_out=hy_w_out, hy_b_out=hy_b_out,
             moe_w_router=moe_w_router, moe_w_gate=moe_w_gate, moe_w_up=moe_w_up, moe_w_down=moe_w_down)
    for name in _BF16_WEIGHTS:
        p[name] = p[name].astype(BF16)
    return (_encoder(x_prompt, p), _encoder(x_sample, p))
```

```python
import functools
import math

import jax
import jax.numpy as jnp
from jax import lax
from jax.experimental import pallas as pl
from jax.experimental.pallas import tpu as pltpu

F32 = jnp.float32
BF16 = jnp.bfloat16
I32 = jnp.int32

RMS_EPS = 1e-6
LN_EPS = 1e-5
CONV_WIDTH = 31
SHORT_WIDTH = 3
EMB_DIM = 33
N_INNER_MLPS = 2
FAST_DECAY_PCT = 0.3
SLOW_DECAY_PCT = 1.5
DECAY_TARGET = 1e-2
N_EXPERTS = 16
CAPACITY_FACTOR = 2

LANE = 128
BF16_SUBLANE_TILE = 16
MXU_DIM = 256
VMEM_LIMIT = 56 * 1024 * 1024

ROW_TILE = 512
DFT_B = 128
GATHER_BLK = 256
GATHER_WIN = GATHER_BLK + BF16_SUBLANE_TILE
COMB_BLK = 512
COMB_WIN = 128
FFN_ROWS = 512


def _cparams(sem):
    return pltpu.CompilerParams(dimension_semantics=sem, vmem_limit_bytes=VMEM_LIMIT)


def _rms(x, g):
    return x * lax.rsqrt(jnp.mean(x * x, axis=-1, keepdims=True) + RMS_EPS) * g


def _norm_proj_kernel(x_ref, g_ref, w_ref, b_ref, o_ref):
    h = _rms(x_ref[...], g_ref[...]).astype(BF16)
    o_ref[...] = jnp.dot(h, w_ref[...], preferred_element_type=F32) + b_ref[...]


def _norm_glu_kernel(x_ref, g_ref, wa_ref, wg_ref, ba_ref, bg_ref, o_ref):
    h = _rms(x_ref[...], g_ref[...]).astype(BF16)
    a = jnp.dot(h, wa_ref[...], preferred_element_type=F32) + ba_ref[...]
    g = jnp.dot(h, wg_ref[...], preferred_element_type=F32) + bg_ref[...]
    o_ref[...] = a * jax.nn.sigmoid(g)


def _norm_proj(x, g, w, b, tn):
    t, d = x.shape
    n = w.shape[1]
    return pl.pallas_call(
        _norm_proj_kernel,
        grid=(t // ROW_TILE, n // tn),
        in_specs=[
            pl.BlockSpec((ROW_TILE, d), lambda i, j: (i, 0)),
            pl.BlockSpec((1, d), lambda i, j: (0, 0)),
            pl.BlockSpec((d, tn), lambda i, j: (0, j)),
            pl.BlockSpec((1, tn), lambda i, j: (0, j)),
        ],
        out_specs=pl.BlockSpec((ROW_TILE, tn), lambda i, j: (i, j)),
        out_shape=jax.ShapeDtypeStruct((t, n), F32),
        compiler_params=_cparams(("parallel", "parallel")),
        name="norm_proj",
    )(x, g.reshape(1, d), w, b.reshape(1, n))


def _norm_glu(x, g, w, b):
    t, d = x.shape
    n = w.shape[1] // 2
    return pl.pallas_call(
        _norm_glu_kernel,
        grid=(t // ROW_TILE,),
        in_specs=[
            pl.BlockSpec((ROW_TILE, d), lambda i: (i, 0)),
            pl.BlockSpec((1, d), lambda i: (0, 0)),
            pl.BlockSpec((d, n), lambda i: (0, 0)),
            pl.BlockSpec((d, n), lambda i: (0, 1)),
            pl.BlockSpec((1, n), lambda i: (0, 0)),
            pl.BlockSpec((1, n), lambda i: (0, 1)),
        ],
        out_specs=pl.BlockSpec((ROW_TILE, n), lambda i: (i, 0)),
        out_shape=jax.ShapeDtypeStruct((t, n), F32),
        compiler_params=_cparams(("parallel",)),
        name="norm_glu",
    )(x, g.reshape(1, d), w, w, b.reshape(1, 2 * n), b.reshape(1, 2 * n))


HALO = 16


CONV_CHUNK = 16


def _conf_tail_kernel(u_ref, up_ref, un_ref, x_ref, wdw_ref, bdw_ref, lng_ref, lnb_ref,
                      w2_ref, b2_ref, o_ref, ext_ref, acc_ref, *, tm, nblk):
    i = pl.program_id(1)
    d = u_ref.shape[-1]
    sl = d // LANE
    prev = up_ref[...].reshape(HALO, sl, LANE)
    nxt = un_ref[...].reshape(HALO, sl, LANE)
    ext_ref[pl.ds(0, HALO)] = jnp.where(i > 0, prev, jnp.zeros_like(prev))
    ext_ref[pl.ds(HALO, tm)] = u_ref[...].reshape(tm, sl, LANE)
    ext_ref[pl.ds(HALO + tm, HALO)] = jnp.where(i < nblk - 1, nxt, jnp.zeros_like(nxt))
    off = HALO - CONV_WIDTH // 2

    def chunk(c, carry):
        r0 = pl.multiple_of(c * CONV_CHUNK, CONV_CHUNK)
        acc = jnp.broadcast_to(bdw_ref[...], (CONV_CHUNK, sl, LANE))
        for k in range(CONV_WIDTH):
            acc = acc + ext_ref[pl.ds(r0 + off + k, CONV_CHUNK)] * wdw_ref[pl.ds(k, 1)]
        acc_ref[pl.ds(r0, CONV_CHUNK)] = acc
        return carry

    lax.fori_loop(0, tm // CONV_CHUNK, chunk, 0)
    acc = acc_ref[...].reshape(tm, d)
    mu = jnp.mean(acc, axis=-1, keepdims=True)
    xc = acc - mu
    var = jnp.mean(xc * xc, axis=-1, keepdims=True)
    y = xc * lax.rsqrt(var + LN_EPS) * lng_ref[...] + lnb_ref[...]
    y = (y * jax.nn.sigmoid(y)).astype(BF16)
    o_ref[...] = jnp.dot(y, w2_ref[...], preferred_element_type=F32) + b2_ref[...] + x_ref[...]


def _conf_tail(u, x, w_dw, b_dw, ln_g, ln_b, w2, b2):
    bt, l, d = u.shape
    tm = ROW_TILE
    nblk = l // tm
    hb = tm // HALO
    sl = d // LANE
    wpad = jnp.zeros((32, d), F32).at[:CONV_WIDTH].set(w_dw).reshape(32, sl, LANE)
    row = lambda a: a.reshape(1, d)
    return pl.pallas_call(
        functools.partial(_conf_tail_kernel, tm=tm, nblk=nblk),
        grid=(bt, nblk),
        in_specs=[
            pl.BlockSpec((None, tm, d), lambda b, i: (b, i, 0)),
            pl.BlockSpec((None, HALO, d), lambda b, i: (b, jnp.maximum(i * hb - 1, 0), 0)),
            pl.BlockSpec((None, HALO, d), lambda b, i: (b, jnp.minimum((i + 1) * hb, nblk * hb - 1), 0)),
            pl.BlockSpec((None, tm, d), lambda b, i: (b, i, 0)),
            pl.BlockSpec((32, sl, LANE), lambda b, i: (0, 0, 0)),
            pl.BlockSpec((1, sl, LANE), lambda b, i: (0, 0, 0)),
            pl.BlockSpec((1, d), lambda b, i: (0, 0)),
            pl.BlockSpec((1, d), lambda b, i: (0, 0)),
            pl.BlockSpec((d, d), lambda b, i: (0, 0)),
            pl.BlockSpec((1, d), lambda b, i: (0, 0)),
        ],
        out_specs=pl.BlockSpec((None, tm, d), lambda b, i: (b, i, 0)),
        out_shape=jax.ShapeDtypeStruct((bt, l, d), F32),
        scratch_shapes=[pltpu.VMEM((tm + 2 * HALO, sl, LANE), F32), pltpu.VMEM((tm, sl, LANE), F32)],
        compiler_params=_cparams(("parallel", "parallel")),
        name="conf_tail",
    )(u, u, u, x, wpad, b_dw.reshape(1, sl, LANE), row(ln_g), row(ln_b), w2, row(b2))


SHALO = 8


def _short_conv_kernel(u_ref, up_ref, un_ref, w_ref, b_ref, vg_ref, vgm_ref, x0_ref, ext_ref, *, tm, nblk, d):
    i = pl.program_id(1)
    prev = up_ref[...]
    nxt = un_ref[...]
    ext_ref[pl.ds(0, SHALO), :] = jnp.where(i > 0, prev, jnp.zeros_like(prev))
    ext_ref[pl.ds(SHALO, tm), :] = u_ref[...]
    ext_ref[pl.ds(SHALO + tm, SHALO), :] = jnp.where(i < nblk - 1, nxt, jnp.zeros_like(nxt))

    def conv(c):
        cols = pl.ds(c * d, d)
        acc = b_ref[:, cols]
        for k in range(SHORT_WIDTH):
            acc = acc + ext_ref[pl.ds(SHALO - 1 + k, tm), cols] * w_ref[pl.ds(k, 1), cols]
        return acc

    x0_ref[...] = conv(0)
    vg = conv(2) * conv(1)
    vg_ref[...] = vg
    _to_group_major(vg, vgm_ref, tm)


DFT_G = DFT_B // 8


def _to_group_major(val, ref, tm):
    for ai in range(tm // DFT_B):
        for g in range(DFT_G):
            ref[g, pl.ds(ai * 8, 8), :] = val[ai * DFT_B + g * 8: ai * DFT_B + g * 8 + 8, :]


def _from_group_major(ref, tm):
    return jnp.concatenate([ref[g, pl.ds(ai * 8, 8), :] for ai in range(tm // DFT_B) for g in range(DFT_G)],
                           axis=0)


def _short_conv(u3, w_short, b_short):
    bt, l, d3 = u3.shape
    d = d3 // 3
    tm = ROW_TILE
    nblk = l // tm
    hb = tm // SHALO
    gm_rows = 8 * tm // DFT_B
    wpad = jnp.zeros((8, d3), F32).at[:SHORT_WIDTH].set(w_short)
    out = jax.ShapeDtypeStruct((bt, l, d), F32)
    return pl.pallas_call(
        functools.partial(_short_conv_kernel, tm=tm, nblk=nblk, d=d),
        grid=(bt, nblk),
        in_specs=[
            pl.BlockSpec((None, tm, d3), lambda b, i: (b, i, 0)),
            pl.BlockSpec((None, SHALO, d3), lambda b, i: (b, jnp.maximum(i * hb - 1, 0), 0)),
            pl.BlockSpec((None, SHALO, d3), lambda b, i: (b, jnp.minimum((i + 1) * hb, nblk * hb - 1), 0)),
            pl.BlockSpec((8, d3), lambda b, i: (0, 0)),
            pl.BlockSpec((1, d3), lambda b, i: (0, 0)),
        ],
        out_specs=[
            pl.BlockSpec((None, tm, d), lambda b, i: (b, i, 0)),
            pl.BlockSpec((None, DFT_G, gm_rows, d), lambda b, i: (b, 0, i, 0)),
            pl.BlockSpec((None, tm, d), lambda b, i: (b, i, 0)),
        ],
        out_shape=[out, jax.ShapeDtypeStruct((bt, DFT_G, 8 * l // DFT_B, d), F32), out],
        scratch_shapes=[pltpu.VMEM((tm + 2 * SHALO, d3), F32)],
        compiler_params=_cparams(("parallel", "parallel")),
        name="short_conv",
    )(u3, u3, u3, wpad, b_short.reshape(1, d3))


_HI16 = -65536


def _pack_complex(re, im):
    hi = lax.bitcast_convert_type(re.astype(BF16).astype(F32), I32)
    lo = lax.bitcast_convert_type(im.astype(BF16).astype(F32), I32)
    return hi | lax.shift_right_logical(lo, jnp.int32(16))


def _unpack_complex(w):
    re = lax.bitcast_convert_type(w & jnp.int32(_HI16), F32)
    im = lax.bitcast_convert_type(lax.shift_left(w, jnp.int32(16)), F32)
    return re.astype(BF16), im.astype(BF16)


def _dft1_kernel(v_ref, fa_ref, x_ref, *, a):
    r = jnp.dot(fa_ref[...], v_ref[...].astype(BF16), preferred_element_type=F32)
    x_ref[...] = _pack_complex(r[:8 * a], r[8 * a:])


def _dft2_kernel(x_ref, mf_ref, mi_ref, kr_ref, ki_ref, z_ref, *, at, b):
    d = x_ref.shape[-1]
    for j in range(at):
        xr, xi = _unpack_complex(x_ref[:, j].reshape(b, d))
        v = jnp.dot(mf_ref[j], jnp.concatenate([xr, xi], axis=0), preferred_element_type=F32)
        vr, vi = v[:b], v[b:]
        kr, ki = kr_ref[j], ki_ref[j]
        yr = (vr * kr - vi * ki).astype(BF16)
        yi = (vr * ki + vi * kr).astype(BF16)
        z = jnp.dot(mi_ref[j], jnp.concatenate([yr, yi], axis=0), preferred_element_type=F32)
        z_ref[:, j] = _pack_complex(z[:b], z[b:]).reshape(DFT_G, 8, d)


def _dft3_kernel(z_ref, fi_ref, y_ref):
    zr, zi = _unpack_complex(z_ref[...])
    y_ref[...] = jnp.dot(fi_ref[...], jnp.concatenate([zr, zi], axis=0), preferred_element_type=F32)


def _dft_constants(l):
    n = 2 * l
    b = DFT_B
    a = n // b
    ah = a // 2
    two_pi = 2.0 * math.pi
    f1 = jnp.arange(a, dtype=F32)[:, None]
    aa = jnp.arange(ah, dtype=F32)[None, :]
    ang_a = two_pi * ((f1 * aa) % a) / a
    eye8 = jnp.eye(8, dtype=F32)
    fa = jnp.kron(jnp.concatenate([jnp.cos(ang_a), -jnp.sin(ang_a)], axis=0), eye8).astype(BF16)
    fi = jnp.kron(jnp.concatenate([jnp.cos(ang_a).T, -jnp.sin(ang_a).T], axis=1), eye8).astype(BF16)
    bb = jnp.arange(b, dtype=F32)
    ang_b = two_pi * ((bb[:, None] * bb[None, :]) % b) / b
    cb, sb = jnp.cos(ang_b)[None], jnp.sin(ang_b)[None]
    ang_t = two_pi * (jnp.arange(a, dtype=F32)[:, None] * bb[None, :]) / n
    ct, st = jnp.cos(ang_t), jnp.sin(ang_t)
    block = lambda re, im: jnp.concatenate([jnp.concatenate([re, -im], 2), jnp.concatenate([im, re], 2)], 1)
    tc, ts = ct[:, None, :], st[:, None, :]
    mf = block(cb * tc - sb * ts, -(cb * ts + sb * tc)).astype(BF16)
    rc, rs = ct[:, :, None], st[:, :, None]
    mi = block(rc * cb - rs * sb, rc * sb + rs * cb).astype(BF16)
    return dict(a=a, ah=ah, fa=fa, fi=fi, mf=mf, mi=mi)


def _dft_stage1(vgm, c):
    bt, _, _, d = vgm.shape
    a, ah = c["a"], c["ah"]
    return pl.pallas_call(
        functools.partial(_dft1_kernel, a=a),
        grid=(bt, DFT_G),
        in_specs=[
            pl.BlockSpec((None, None, 8 * ah, d), lambda n_, g: (n_, g, 0, 0)),
            pl.BlockSpec((16 * a, 8 * ah), lambda n_, g: (0, 0)),
        ],
        out_specs=pl.BlockSpec((None, None, 8 * a, d), lambda n_, g: (n_, g, 0, 0)),
        out_shape=jax.ShapeDtypeStruct((bt, DFT_G, 8 * a, d), I32),
        compiler_params=_cparams(("parallel", "parallel")),
        name="dft_stage1",
    )(vgm, c["fa"])


SPEC_F1 = 4


def _filter_spec_kernel(x_ref, mf_ref, s_ref, kr_ref, ki_ref, *, b):
    d = x_ref.shape[-1]
    for j in range(SPEC_F1):
        spec = []
        for side in range(2):
            xr, xi = _unpack_complex(x_ref[side, :, j].reshape(b, d))
            spec.append(jnp.dot(mf_ref[j], jnp.concatenate([xr, xi], axis=0), preferred_element_type=F32))
        fwd, bwd = spec
        kr_ref[j] = (fwd[:b] + bwd[:b]) * s_ref[...]
        ki_ref[j] = (fwd[b:] - bwd[b:]) * s_ref[...]


def _filter_spectrum(hgm, scale, c):
    d = hgm.shape[-1]
    a, b = c["a"], DFT_B
    x = _dft_stage1(hgm, c)
    out = jax.ShapeDtypeStruct((a, b, d), F32)
    return pl.pallas_call(
        functools.partial(_filter_spec_kernel, b=b),
        grid=(a // SPEC_F1,),
        in_specs=[
            pl.BlockSpec((2, DFT_G, SPEC_F1, 8, d), lambda i: (0, 0, i, 0, 0)),
            pl.BlockSpec((SPEC_F1, 2 * b, 2 * b), lambda i: (i, 0, 0)),
            pl.BlockSpec((1, d), lambda i: (0, 0)),
        ],
        out_specs=[pl.BlockSpec((SPEC_F1, b, d), lambda i: (i, 0, 0))] * 2,
        out_shape=[out, out],
        compiler_params=_cparams(("parallel",)),
        name="filter_spectrum",
    )(x.reshape(2, DFT_G, a, 8, d), c["mf"], scale.reshape(1, d))


def _long_conv(vgm, kr, ki, c):
    bt, _, _, d = vgm.shape
    a, ah, b = c["a"], c["ah"], DFT_B
    mf, mi, fi = c["mf"], c["mi"], c["fi"]
    x = _dft_stage1(vgm, c)

    at = SPEC_F1
    z = pl.pallas_call(
        functools.partial(_dft2_kernel, at=at, b=b),
        grid=(a // at, bt),
        in_specs=[
            pl.BlockSpec((None, DFT_G, at, 8, d), lambda i, n_: (n_, 0, i, 0, 0)),
            pl.BlockSpec((at, 2 * b, 2 * b), lambda i, n_: (i, 0, 0)),
            pl.BlockSpec((at, 2 * b, 2 * b), lambda i, n_: (i, 0, 0)),
            pl.BlockSpec((at, b, d), lambda i, n_: (i, 0, 0)),
            pl.BlockSpec((at, b, d), lambda i, n_: (i, 0, 0)),
        ],
        out_specs=pl.BlockSpec((None, DFT_G, at, 8, d), lambda i, n_: (n_, 0, i, 0, 0)),
        out_shape=jax.ShapeDtypeStruct((bt, DFT_G, a, 8, d), I32),
        compiler_params=_cparams(("parallel", "parallel")),
        name="dft_stage2",
    )(x.reshape(bt, DFT_G, a, 8, d), mf, mi, kr, ki)

    return pl.pallas_call(
        _dft3_kernel,
        grid=(bt, DFT_G),
        in_specs=[
            pl.BlockSpec((None, None, 8 * a, d), lambda n_, g: (n_, g, 0, 0)),
            pl.BlockSpec((8 * ah, 16 * a), lambda n_, g: (0, 0)),
        ],
        out_specs=pl.BlockSpec((None, None, 8 * ah, d), lambda n_, g: (n_, g, 0, 0)),
        out_shape=jax.ShapeDtypeStruct((bt, DFT_G, 8 * ah, d), F32),
        compiler_params=_cparams(("parallel", "parallel")),
        name="dft_stage3",
    )(z.reshape(bt, DFT_G, 8 * a, d), fi)


def _filter_taps_kernel(fv_ref, w0_ref, b0_ref, win_ref, bin_ref, wout_ref, freq_ref, dl_ref,
                        h_ref, s_ref, *, tm, l, d):
    hp = lax.Precision.HIGHEST
    n = (pl.program_id(0) * tm + lax.broadcasted_iota(I32, (tm, 1), 0)).astype(F32)
    t = n / (l - 1)
    arg = (2.0 * math.pi / l) * n * fv_ref[...]
    lane = lax.broadcasted_iota(I32, (tm, LANE), 1)
    bands = (EMB_DIM - 1) // 2
    z = jnp.where(lane == 0, t, jnp.where(lane <= bands, jnp.cos(arg),
                                          jnp.where(lane <= 2 * bands, -jnp.sin(arg), 0.0)))
    freq = freq_ref[...]
    hf = jnp.sin(freq * (jnp.dot(z, w0_ref[...], precision=hp, preferred_element_type=F32) + b0_ref[...]))
    for i in range(N_INNER_MLPS):
        hf = jnp.sin(freq * (jnp.dot(hf, win_ref[i], precision=hp, preferred_element_type=F32) + bin_ref[i]))
    hf = jnp.dot(hf, wout_ref[...], precision=hp, preferred_element_type=F32)
    decay = jnp.exp(-t * jnp.abs(dl_ref[...]))
    h_fwd = hf[:, :d] * decay
    h_bwd0 = jnp.where(n > 0.0, hf[:, d:] * decay, 0.0)
    _to_group_major(h_fwd, h_ref.at[0], tm)
    _to_group_major(h_bwd0, h_ref.at[1], tm)
    part = jnp.sum(jnp.abs(h_fwd) + jnp.abs(h_bwd0), axis=0, keepdims=True)
    s_ref[...] = jnp.broadcast_to(part, s_ref.shape)


def _hyena_filter_taps(l, w_f0, b_f0, w_fin, b_fin, w_fout, sin_freq):
    d = w_fout.shape[-1] // 2
    o = w_f0.shape[1]
    tm = ROW_TILE
    nblk = l // tm
    bands = (EMB_DIM - 1) // 2
    f = jnp.linspace(1e-4, bands - 1, bands, dtype=F32)
    fvec = jnp.zeros((1, LANE), F32).at[0, 1:1 + bands].set(f).at[0, 1 + bands:1 + 2 * bands].set(f)
    pad = lambda a, shape: jnp.zeros(shape, F32).at[tuple(slice(0, s) for s in a.shape)].set(a)
    max_decay = math.log(DECAY_TARGET) / FAST_DECAY_PCT
    min_decay = math.log(DECAY_TARGET) / SLOW_DECAY_PCT
    deltas = jnp.linspace(min_decay, max_decay, d, dtype=F32)[None, :]
    full = lambda *shape: pl.BlockSpec(shape, lambda i: (0,) * len(shape))
    hgm, sums = pl.pallas_call(
        functools.partial(_filter_taps_kernel, tm=tm, l=l, d=d),
        grid=(nblk,),
        in_specs=[full(1, LANE), full(LANE, LANE), full(1, LANE), full(N_INNER_MLPS, LANE, LANE),
                  full(N_INNER_MLPS, 1, LANE), full(LANE, 2 * d), full(1, LANE), full(1, d)],
        out_specs=[pl.BlockSpec((2, DFT_G, 8 * tm // DFT_B, d), lambda i: (0, 0, i, 0)),
                   pl.BlockSpec((None, 8, d), lambda i: (i, 0, 0))],
        out_shape=[jax.ShapeDtypeStruct((2, DFT_G, 8 * l // DFT_B, d), F32),
                   jax.ShapeDtypeStruct((nblk, 8, d), F32)],
        compiler_params=_cparams(("parallel",)),
        name="filter_taps",
    )(fvec, pad(w_f0, (LANE, LANE)), pad(b_f0[None], (1, LANE)), pad(w_fin, (N_INNER_MLPS, LANE, LANE)),
      pad(b_fin[:, None], (N_INNER_MLPS, 1, LANE)), pad(w_fout, (LANE, 2 * d)), pad(sin_freq[None], (1, LANE)), deltas)
    assert o <= LANE and EMB_DIM <= LANE
    return hgm, 1.0 / (jnp.sum(sums[:, 0, :], axis=0) * (2 * l))


def _hyena_tail_kernel(y_ref, v_ref, x0_ref, fb_ref, w_ref, b_ref, x_ref, o_ref, *, tm):
    y = _from_group_major(y_ref, tm)
    g = ((y + v_ref[...] * fb_ref[...]) * x0_ref[...]).astype(BF16)
    o_ref[...] = jnp.dot(g, w_ref[...], preferred_element_type=F32) + b_ref[...] + x_ref[...]


def _hyena_tail(ygm, v, x0, fft_bias, w_out, b_out, x):
    bt, l, d = x.shape
    tm = ROW_TILE
    tile = pl.BlockSpec((None, tm, d), lambda b, i: (b, i, 0))
    rowv = pl.BlockSpec((1, d), lambda b, i: (0, 0))
    return pl.pallas_call(
        functools.partial(_hyena_tail_kernel, tm=tm),
        grid=(bt, l // tm),
        in_specs=[pl.BlockSpec((None, DFT_G, 8 * tm // DFT_B, d), lambda b, i: (b, 0, i, 0)),
                  tile, tile, rowv, pl.BlockSpec((d, d), lambda b, i: (0, 0)), rowv, tile],
        out_specs=tile,
        out_shape=jax.ShapeDtypeStruct((bt, l, d), F32),
        compiler_params=_cparams(("parallel", "parallel")),
        name="hyena_tail",
    )(ygm, v, x0, fft_bias.reshape(1, d), w_out, b_out.reshape(1, d), x)


def _router_kernel(x_ref, g_ref, wt_ref, hn_ref, pt_ref):
    h = _rms(x_ref[...], g_ref[...])
    hn_ref[...] = h.astype(BF16)
    lt = lax.dot_general(wt_ref[...], h, (((1,), (1,)), ((), ())), precision=lax.Precision.HIGHEST,
                         preferred_element_type=F32)
    et = jnp.exp(lt - jnp.max(lt, axis=0, keepdims=True))
    pt_ref[...] = et / jnp.sum(et, axis=0, keepdims=True)


def _router(x, g, w_router):
    t, d = x.shape
    e = w_router.shape[1]
    return pl.pallas_call(
        _router_kernel,
        grid=(t // ROW_TILE,),
        in_specs=[
            pl.BlockSpec((ROW_TILE, d), lambda i: (i, 0)),
            pl.BlockSpec((1, d), lambda i: (0, 0)),
            pl.BlockSpec((e, d), lambda i: (0, 0)),
        ],
        out_specs=[
            pl.BlockSpec((ROW_TILE, d), lambda i: (i, 0)),
            pl.BlockSpec((e, ROW_TILE), lambda i: (0, i)),
        ],
        out_shape=[
            jax.ShapeDtypeStruct((t, d), BF16),
            jax.ShapeDtypeStruct((e, t), F32),
        ],
        compiler_params=_cparams(("parallel",)),
        name="router",
    )(x, g.reshape(1, d), w_router.T)


CUM_BLK = 256


def _select_kernel(p_ref, posm_ref, posx_ref, mask_ref, *, cap, t):
    keys = pltpu.bitcast(p_ref[...], I32)
    e = keys.shape[0]

    def vbit(i, tau):
        cand = tau | jnp.left_shift(jnp.int32(1), 30 - i)
        cnt = jnp.sum((keys >= cand).astype(I32), axis=1, keepdims=True)
        return jnp.where(cnt >= cap, cand, tau)

    tau = lax.fori_loop(0, 31, vbit, jnp.zeros((e, 1), I32))
    gt = keys > tau
    eq = keys == tau
    quota = cap - jnp.sum(gt.astype(I32), axis=1, keepdims=True)
    idx = lax.broadcasted_iota(I32, keys.shape, 1)
    nbits = max(1, (t - 1).bit_length())

    def ibit(i, jj):
        cand = jj | jnp.left_shift(jnp.int32(1), nbits - 1 - i)
        cnt = jnp.sum((eq & (idx < cand)).astype(I32), axis=1, keepdims=True)
        return jnp.where(cnt < quota, cand, jj)

    jlast = lax.fori_loop(0, nbits, ibit, jnp.zeros((e, 1), I32))
    mask = gt | (eq & (idx <= jlast))
    mask_ref[...] = mask.astype(F32)

    r = lax.broadcasted_iota(I32, (CUM_BLK, CUM_BLK), 0)
    c = lax.broadcasted_iota(I32, (CUM_BLK, CUM_BLK), 1)
    tri = (r <= c).astype(BF16)
    run = jnp.zeros((e, 1), F32)
    for blk in range(t // CUM_BLK):
        cols = pl.ds(blk * CUM_BLK, CUM_BLK)
        m = mask_ref[:, cols]
        incl = jnp.dot(m.astype(BF16), tri, preferred_element_type=F32)
        pos = (incl - m + run).astype(I32)
        posx_ref[:, cols] = pos
        posm_ref[:, cols] = jnp.where(m > 0.0, pos, -1)
        run = run + incl[:, CUM_BLK - 1:CUM_BLK]


def _select(probs_t, cap):
    e, t = probs_t.shape
    full = pl.BlockSpec((e, t), lambda: (0, 0))
    return pl.pallas_call(
        functools.partial(_select_kernel, cap=cap, t=t),
        in_specs=[full],
        out_specs=[full, full],
        out_shape=[jax.ShapeDtypeStruct((e, t), I32), jax.ShapeDtypeStruct((e, t), I32)],
        scratch_shapes=[pltpu.VMEM((e, t), F32)],
        compiler_params=pltpu.CompilerParams(vmem_limit_bytes=VMEM_LIMIT),
        name="select",
    )(probs_t)


GATHER_EXPERTS = 2


GATHER_ROWS = 64


def _gather_kernel(st_ref, nch_ref, hn_ref, posm_ref, p_ref, o_ref, g_ref, *, sub):
    eg = pl.program_id(0)
    k = pl.program_id(1)

    @pl.when(k == 0)
    def _():
        o_ref[...] = jnp.zeros_like(o_ref)
        g_ref[...] = jnp.zeros_like(g_ref)

    rows = lax.broadcasted_iota(I32, (GATHER_ROWS, GATHER_BLK), 0)

    def onehot(j, x, chunk):
        tok = pl.ds(j * GATHER_BLK, GATHER_BLK)
        base = (pl.multiple_of(st_ref[eg * GATHER_EXPERTS + x, k * sub + j], BF16_SUBLANE_TILE)
                + chunk * GATHER_ROWS)
        hit = rows == posm_ref[x, :, tok] - base
        gate = jnp.sum(jnp.where(hit, p_ref[x, :, tok], 0.0), axis=1, keepdims=True)
        return base, hit.astype(BF16), gate

    def add_rows(x, base, slab, gate):
        win = pl.ds(pl.multiple_of(base, BF16_SUBLANE_TILE), GATHER_ROWS)
        o_ref[x, win, :] = o_ref[x, win, :] + slab.astype(BF16)
        g_ref[x, win, :] = g_ref[x, win, :] + gate

    for j in range(sub):
        first = [onehot(j, x, 0) for x in range(GATHER_EXPERTS)]
        slab = jnp.dot(jnp.concatenate([oh for _, oh, _ in first], axis=0),
                       hn_ref[pl.ds(j * GATHER_BLK, GATHER_BLK), :], preferred_element_type=F32)
        for x in range(GATHER_EXPERTS):
            add_rows(x, first[x][0], slab[x * GATHER_ROWS:(x + 1) * GATHER_ROWS], first[x][2])

    for j in range(sub):
        for x in range(GATHER_EXPERTS):
            def extra(chunk, carry):
                base, oh, gate = onehot(j, x, chunk)
                add_rows(x, base, jnp.dot(oh, hn_ref[pl.ds(j * GATHER_BLK, GATHER_BLK), :],
                                          preferred_element_type=F32), gate)
                return carry

            lax.fori_loop(1, nch_ref[eg * GATHER_EXPERTS + x, k * sub + j], extra, 0)


def _gather_tokens(hn, posm, probs_t, starts, nchunks, cap):
    t, d = hn.shape
    e = posm.shape[0]
    sub = 8
    nb = t // (GATHER_BLK * sub)
    cp = cap + GATHER_WIN
    return pl.pallas_call(
        functools.partial(_gather_kernel, sub=sub),
        grid_spec=pltpu.PrefetchScalarGridSpec(
            num_scalar_prefetch=2,
            grid=(e // GATHER_EXPERTS, nb),
            in_specs=[
                pl.BlockSpec((GATHER_BLK * sub, d), lambda ei, k, st, nc: (k, 0)),
                pl.BlockSpec((GATHER_EXPERTS, 1, GATHER_BLK * sub), lambda ei, k, st, nc: (ei, 0, k)),
                pl.BlockSpec((GATHER_EXPERTS, 1, GATHER_BLK * sub), lambda ei, k, st, nc: (ei, 0, k)),
            ],
            out_specs=[pl.BlockSpec((GATHER_EXPERTS, cp, d), lambda ei, k, st, nc: (ei, 0, 0)),
                       pl.BlockSpec((GATHER_EXPERTS, cp, LANE), lambda ei, k, st, nc: (ei, 0, 0))],
        ),
        out_shape=[jax.ShapeDtypeStruct((e, cp, d), BF16), jax.ShapeDtypeStruct((e, cp, LANE), F32)],
        compiler_params=_cparams(("arbitrary", "arbitrary")),
        name="gather_tokens",
    )(starts, nchunks, hn, posm.reshape(e, 1, t), probs_t.reshape(e, 1, t))


def _ffn_kernel(x_ref, gate_ref, wg_ref, wu_ref, wd_ref, o_ref, *, nrt, splits):
    r = pl.program_id(1)

    @pl.when(r < nrt)
    def _():
        x = x_ref[...]
        dn = None
        for c0, c1 in splits:
            g = jnp.dot(x, wg_ref[:, c0:c1], preferred_element_type=F32)
            u = jnp.dot(x, wu_ref[:, c0:c1], preferred_element_type=F32)
            h = (g * jax.nn.sigmoid(g) * u).astype(BF16)
            part = jnp.dot(h, wd_ref[c0:c1, :], preferred_element_type=F32)
            dn = part if dn is None else dn + part
        o_ref[...] = (dn * gate_ref[:, 0:1]).astype(BF16)

    @pl.when(r == nrt)
    def _():
        o_ref[...] = jnp.zeros_like(o_ref)


def _expert_ffn(xs, gates, w_gate, w_up, w_down, layer, cap):
    e, _, d = xs.shape
    f = w_gate.shape[-1]
    nrt = cap // FFN_ROWS
    half = (f // MXU_DIM + 1) // 2 * MXU_DIM
    splits = ((0, half), (half, f))
    row = lambda r: jnp.minimum(r, nrt - 1)
    return pl.pallas_call(
        functools.partial(_ffn_kernel, nrt=nrt, splits=splits),
        grid=(e, nrt + 1),
        in_specs=[
            pl.BlockSpec((None, FFN_ROWS, d), lambda ei, r: (ei, row(r), 0)),
            pl.BlockSpec((None, FFN_ROWS, LANE), lambda ei, r: (ei, row(r), 0)),
            pl.BlockSpec((None, None, d, f), lambda ei, r: (layer, ei, 0, 0)),
            pl.BlockSpec((None, None, d, f), lambda ei, r: (layer, ei, 0, 0)),
            pl.BlockSpec((None, None, f, d), lambda ei, r: (layer, ei, 0, 0)),
        ],
        out_specs=pl.BlockSpec((None, FFN_ROWS, d), lambda ei, r: (ei, r, 0)),
        out_shape=jax.ShapeDtypeStruct((e, cap + FFN_ROWS, d), BF16),
        compiler_params=_cparams(("parallel", "arbitrary")),
        name="expert_ffn",
    )(xs, gates, w_gate, w_up, w_down)


def _combine_kernel(st_ref, nch_ref, x_ref, posm_ref, g_ref, out_hbm, o_ref,
                    stack, lhs, xbuf, sem, xsem, *, ne, cp, final_norm):
    b = pl.program_id(0)
    nsteps = pl.num_programs(0)

    def window_copies(step, slot):
        for ei in range(ne):
            start = pl.multiple_of(ei * cp + st_ref[ei, step], BF16_SUBLANE_TILE)
            dst = stack.at[slot, pl.ds(ei * COMB_WIN, COMB_WIN), :]
            yield pltpu.make_async_copy(out_hbm.at[pl.ds(start, COMB_WIN), :], dst, sem.at[slot])

    @pl.when(b == 0)
    def _():
        for c in window_copies(0, 0):
            c.start()

    @pl.when(b + 1 < nsteps)
    def _():
        for c in window_copies(b + 1, (b + 1) % 2):
            c.start()

    cols = lax.broadcasted_iota(I32, (COMB_BLK, COMB_WIN), 1)

    def onehot(ei, chunk):
        rel = posm_ref[:, ei:ei + 1] - st_ref[ei, b] - chunk * COMB_WIN
        return (cols == rel).astype(BF16)

    for ei in range(ne):
        lhs[:, pl.ds(ei * COMB_WIN, COMB_WIN)] = onehot(ei, 0)

    slot = b % 2
    for c in window_copies(b, slot):
        c.wait()
    o_ref[...] = x_ref[...] + jnp.dot(lhs[...], stack[slot], preferred_element_type=F32)

    for ei in range(ne):
        def extra(chunk, carry):
            start = pl.multiple_of(ei * cp + st_ref[ei, b] + chunk * COMB_WIN, BF16_SUBLANE_TILE)
            cpy = pltpu.make_async_copy(out_hbm.at[pl.ds(start, COMB_WIN), :], xbuf, xsem)
            cpy.start()
            cpy.wait()
            o_ref[...] = o_ref[...] + jnp.dot(onehot(ei, chunk), xbuf[...], preferred_element_type=F32)
            return carry

        lax.fori_loop(1, nch_ref[ei, b], extra, 0)

    if final_norm:
        o_ref[...] = _rms(o_ref[...], g_ref[...])


def _combine(x, posm_tok, out_rows, starts, nchunks, final_g):
    t, d = x.shape
    e = posm_tok.shape[1]
    cp = out_rows.shape[0] // e
    final_norm = final_g is not None
    g = final_g if final_norm else jnp.ones((d,), F32)
    return pl.pallas_call(
        functools.partial(_combine_kernel, ne=e, cp=cp, final_norm=final_norm),
        grid_spec=pltpu.PrefetchScalarGridSpec(
            num_scalar_prefetch=2,
            grid=(t // COMB_BLK,),
            in_specs=[
                pl.BlockSpec((COMB_BLK, d), lambda i, st, nc: (i, 0)),
                pl.BlockSpec((COMB_BLK, e), lambda i, st, nc: (i, 0)),
                pl.BlockSpec((1, d), lambda i, st, nc: (0, 0)),
                pl.BlockSpec(memory_space=pl.ANY),
            ],
            out_specs=pl.BlockSpec((COMB_BLK, d), lambda i, st, nc: (i, 0)),
            scratch_shapes=[
                pltpu.VMEM((2, e * COMB_WIN, d), BF16),
                pltpu.VMEM((COMB_BLK, e * COMB_WIN), BF16),
                pltpu.VMEM((COMB_WIN, d), BF16),
                pltpu.SemaphoreType.DMA((2,)),
                pltpu.SemaphoreType.DMA,
            ],
        ),
        out_shape=jax.ShapeDtypeStruct((t, d), F32),
        compiler_params=_cparams(("arbitrary",)),
        name="combine",
    )(starts, nchunks, x, posm_tok, g.reshape(1, d), out_rows)


def _window_tables(posx, blk, win, cap):
    e = posx.shape[0]
    cum = posx[:, ::blk]
    cnt = jnp.concatenate([cum[:, 1:], jnp.full((e, 1), cap, I32)], axis=1) - cum
    starts = (cum // BF16_SUBLANE_TILE) * BF16_SUBLANE_TILE
    return starts, (cum - starts + cnt + win - 1) // win


def _ec_moe(x, g, w_router, w_gate, w_up, w_down, layer, final_g):
    t, d = x.shape
    e = w_router.shape[1]
    cap = max(1, min(t, CAPACITY_FACTOR * t // e))
    hn, probs_t = _router(x, g, w_router)
    posm, posx = _select(probs_t, cap)
    xs, gates = _gather_tokens(hn, posm, probs_t, *_window_tables(posx, GATHER_BLK, GATHER_ROWS, cap), cap)
    out_rows = _expert_ffn(xs, gates, w_gate, w_up, w_down, layer, cap)
    return _combine(x, posm.T, out_rows.reshape(-1, d), *_window_tables(posx, COMB_BLK, COMB_WIN, cap), final_g)


def _encoder(x3, p):
    bt, l, d = x3.shape
    t = bt * l
    x = x3.reshape(t, d)
    depth = p["norm_mix_g"].shape[0]
    for i in range(depth):
        j = i // 2
        if i % 2 == 0:
            u = _norm_glu(x, p["norm_mix_g"][i], p["conf_w_pw1"][j], p["conf_b_pw1"][j])
            x = _conf_tail(u.reshape(bt, l, d), x.reshape(bt, l, d), p["conf_w_dw"][j], p["conf_b_dw"][j],
                           p["conf_ln_g"][j], p["conf_ln_b"][j], p["conf_w_pw2"][j],
                           p["conf_b_pw2"][j]).reshape(t, d)
        else:
            u3 = _norm_proj(x, p["norm_mix_g"][i], p["hy_w_in"][j], p["hy_b_in"][j], tn=3 * d)
            vg, vgm, x0 = _short_conv(u3.reshape(bt, l, 3 * d), p["hy_w_short"][j], p["hy_b_short"][j])
            c = _dft_constants(l)
            taps = _hyena_filter_taps(l, p["hy_w_f0"][j], p["hy_b_f0"][j], p["hy_w_fin"][j],
                                      p["hy_b_fin"][j], p["hy_w_fout"][j], p["hy_sin_freq"][j])
            kr, ki = _filter_spectrum(*taps, c)
            ygm = _long_conv(vgm, kr, ki, c)
            x = _hyena_tail(ygm, vg, x0, p["hy_fft_bias"][j], p["hy_w_out"][j], p["hy_b_out"][j],
                            x.reshape(bt, l, d)).reshape(t, d)
        final_g = p["final_norm_g"] if i == depth - 1 else None
        x = _ec_moe(x, p["norm_ffn_g"][i], p["moe_w_router"][i], p["moe_w_gate"], p["moe_w_up"],
                    p["moe_w_down"], i, final_g)
    return x.reshape(bt, l, d)


_BF16_WEIGHTS = ("conf_w_pw1", "conf_w_pw2", "hy_w_in", "hy_w_out", "moe_w_gate", "moe_w_up", "moe_w_down")


def kernel(x_prompt, x_sample, norm_mix_g, norm_ffn_g, final_norm_g, conf_w_pw1, conf_b_pw1, conf_w_dw, conf_b_dw, conf_ln_g, conf_ln_b, conf_w_pw2, conf_b_pw2, hy_w_in, hy_b_in, hy_w_short, hy_b_short, hy_w_f0, hy_b_f0, hy_w_fin, hy_b_fin, hy_w_fout, hy_sin_freq, hy_fft_bias, hy_w_out, hy_b_out, moe_w_router, moe_w_gate, moe_w_up, moe_w_down):
    p = dict(norm_mix_g=norm_mix_g, norm_ffn_g=norm_ffn_g, final_norm_g=final_norm_g,
             conf_w_pw1=conf_w_pw1, conf_b_pw1=conf_b_pw1, conf_w_dw=conf_w_dw, conf_b_dw=conf_b_dw,
             conf_ln_g=conf_ln_g, conf_ln_b=conf_ln_b, conf_w_pw2=conf_w_pw2, conf_b_pw2=conf_b_pw2,
             hy_w_in=hy_w_in, hy_b_in=hy_b_in, hy_w_short=hy_w_short, hy_b_short=hy_b_short,
             hy_w_f0=hy_w_f0, hy_b_f0=hy_b_f0, hy_w_fin=hy_w_fin, hy_b_fin=hy_b_fin, hy_w_fout=hy_w_fout,
             hy_sin_freq=hy_sin_freq, hy_fft_bias=hy_fft_bias, hy_w_out=hy_w_out, hy_b_out=hy_b_out,
             moe_w_router=moe_w_router, moe_w_gate=moe_w_gate, moe_w_up=moe_w_up, moe_w_down=moe_w_down)
    for name in _BF16_WEIGHTS:
        p[name] = p[name].astype(BF16)
    return (_encoder(x_prompt, p), _encoder(x_sample, p))
```
